```python
import jax, jax.numpy as jnp
from jax import lax
import numpy as np

D_MODEL = 2048
BATCH = 4
SEQ = 4096
DEPTH = 2

SC_WIDTH = 1024
SC_CONV_WIDTH = 3
HEAD_DIM = 64
N_Q_HEADS = 16
N_KV_HEADS = 4
Q_WIDTH = N_Q_HEADS * HEAD_DIM
KV_WIDTH = N_KV_HEADS * HEAD_DIM
WINDOW = 128
BLOCK = 128
CF_WIDTH = 1024
CF_CONV_WIDTH = 31
N_MEM = 256
X_HEADS = 4
X_HEAD_DIM = 128
X_WIDTH = X_HEADS * X_HEAD_DIM
N_GROUPS = 4
EXPERTS_PER_GROUP = 8
N_EXPERTS = N_GROUPS * EXPERTS_PER_GROUP
TOP_K = 2
D_EXPERT = 512

EPS = 1e-6
NEG_INF = -1e30

IN_SIZES = (SC_WIDTH, SC_WIDTH, SC_WIDTH, Q_WIDTH, KV_WIDTH, KV_WIDTH,
            CF_WIDTH, CF_WIDTH, D_MODEL, D_MODEL, D_MODEL)
IN_COLS = sum(IN_SIZES)
SPLIT_POINTS = tuple(sum(IN_SIZES[:i + 1]) for i in range(len(IN_SIZES) - 1))

kernel_name = "hybrid_gated_conv_swa_conformer_hmoe"


def rms_norm(x, g):
    xf = x.astype(jnp.float32)
    y = xf * lax.rsqrt(jnp.mean(xf * xf, axis=-1, keepdims=True) + EPS)
    return (y * g.astype(jnp.float32)).astype(x.dtype)


def layer_norm(x, g, b):
    xf = x.astype(jnp.float32)
    mu = jnp.mean(xf, axis=-1, keepdims=True)
    xc = xf - mu
    y = xc * lax.rsqrt(jnp.mean(xc * xc, axis=-1, keepdims=True) + EPS)
    return (y * g.astype(jnp.float32) + b.astype(jnp.float32)).astype(x.dtype)


def causal_depthwise_conv(u, w):
    k = w.shape[0]
    return lax.conv_general_dilated(
        u, w[:, None, :].astype(u.dtype), window_strides=(1,), padding=((k - 1, 0),),
        dimension_numbers=("NWC", "WIO", "NWC"), feature_group_count=u.shape[-1])


def sliding_window_attention(q, k, v, q_gain, k_gain, sinks):
    b, s = q.shape[:2]
    nb = s // BLOCK
    grp = N_Q_HEADS // N_KV_HEADS
    q = rms_norm(q, q_gain) * (HEAD_DIM ** -0.5)
    k = rms_norm(k, k_gain)
    qb = q.reshape(b, nb, BLOCK, N_KV_HEADS, grp, HEAD_DIM)

    def with_prev_block(t):
        tb = t.reshape(b, nb, BLOCK, N_KV_HEADS, HEAD_DIM)
        prev = jnp.pad(tb[:, :-1], ((0, 0), (1, 0), (0, 0), (0, 0), (0, 0)))
        return jnp.concatenate([prev, tb], axis=2)

    kb, vb = with_prev_block(k), with_prev_block(v)
    scores = jnp.einsum("bnqhgd,bnjhd->bnhgqj", qb, kb).astype(jnp.float32)
    qpos = jnp.arange(BLOCK)[:, None]
    kpos = jnp.arange(2 * BLOCK)[None, :] - BLOCK
    rel = qpos - kpos
    band = (rel >= 0) & (rel < WINDOW)
    block_start = jnp.arange(nb)[:, None, None] * BLOCK
    valid = band[None] & (block_start + kpos[None] >= 0)
    scores = jnp.where(valid[None, :, None, None], scores, NEG_INF)
    sink = jnp.broadcast_to(sinks.astype(jnp.float32).reshape(1, 1, N_KV_HEADS, grp, 1, 1),
                            scores.shape[:-1] + (1,))
    probs = jax.nn.softmax(jnp.concatenate([scores, sink], axis=-1), axis=-1)[..., :-1]
    out = jnp.einsum("bnhgqj,bnjhd->bnqhgd", probs.astype(v.dtype), vb)
    return out.reshape(b, s, Q_WIDTH)


def memory_cross_attention(h, mem_n, wq, wkv, q_gain, k_gain, wo):
    b, s, _ = h.shape
    m = mem_n.shape[1]
    q = (h @ wq).reshape(b, s, X_HEADS, X_HEAD_DIM)
    k, v = jnp.split(mem_n @ wkv, 2, axis=-1)
    k = k.reshape(b, m, X_HEADS, X_HEAD_DIM)
    v = v.reshape(b, m, X_HEADS, X_HEAD_DIM)
    q = rms_norm(q, q_gain) * (X_HEAD_DIM ** -0.5)
    k = rms_norm(k, k_gain)
    scores = jnp.einsum("bshd,bmhd->bhsm", q, k).astype(jnp.float32)
    probs = jax.nn.softmax(scores, axis=-1).astype(v.dtype)
    o = jnp.einsum("bhsm,bmhd->bshd", probs, v).reshape(b, s, X_WIDTH)
    return o @ wo


def hierarchical_moe(h, w_rg, b_rg, w_re, b_re, w_gu, w_dn):
    b, s, d = h.shape
    t = h.reshape(b * s, d)
    n_tok = t.shape[0]
    p_group = jax.nn.softmax((t @ w_rg).astype(jnp.float32) + b_rg.astype(jnp.float32), axis=-1)
    p_g, g_idx = lax.top_k(p_group, 1)
    e_logits = ((t @ w_re).astype(jnp.float32) + b_re.astype(jnp.float32)).reshape(
        n_tok, N_GROUPS, EXPERTS_PER_GROUP)
    e_in = jnp.take_along_axis(e_logits, g_idx[:, :, None], axis=1)[:, 0]
    p_e = jax.nn.softmax(e_in, axis=-1)
    top_p, top_i = lax.top_k(p_e, TOP_K)
    top_p = top_p / jnp.sum(top_p, axis=-1, keepdims=True)
    w = p_g * top_p
    eid = g_idx * EXPERTS_PER_GROUP + top_i
    combine = jnp.sum(jax.nn.one_hot(eid, N_EXPERTS, dtype=jnp.float32) * w[..., None], axis=1)
    combine = combine.astype(t.dtype)
    y = jnp.zeros_like(t)
    for e in range(N_EXPERTS):
        gate, up = jnp.split(t @ w_gu[e], 2, axis=-1)
        y = y + combine[:, e:e + 1] * ((jax.nn.silu(gate) * up) @ w_dn[e])
    return y.reshape(b, s, d)


def setup_inputs(seed: int = 0) -> dict:
    key = jax.random.key(seed)
    ks = iter(jax.random.split(key, 32))
    f32 = jnp.float32
    L = DEPTH

    def dense(shape, fan_in):
        return jax.random.normal(next(ks), shape, f32) * (fan_in ** -0.5)

    def gain(shape):
        return 1.0 + 0.01 * jax.random.normal(next(ks), shape, f32)

    def small(shape, scale=0.01):
        return scale * jax.random.normal(next(ks), shape, f32)

    return {
        "x": jax.random.normal(next(ks), (BATCH, SEQ, D_MODEL), f32),
        "mem": jax.random.normal(next(ks), (BATCH, N_MEM, D_MODEL), f32),
        "mix_norm": gain((L, D_MODEL)),
        "w_in": dense((L, D_MODEL, IN_COLS), D_MODEL),
        "sc_conv": dense((L, SC_CONV_WIDTH, SC_WIDTH), SC_CONV_WIDTH),
        "attn_q_gain": gain((L, HEAD_DIM)),
        "attn_k_gain": gain((L, HEAD_DIM)),
        "attn_sinks": small((L, N_Q_HEADS), 0.5),
        "w_out_sc": dense((L, SC_WIDTH, D_MODEL), SC_WIDTH),
        "w_out_attn": dense((L, Q_WIDTH, D_MODEL), Q_WIDTH),
        "cf_conv": dense((L, CF_CONV_WIDTH, CF_WIDTH), CF_CONV_WIDTH),
        "cf_conv_b": small((L, CF_WIDTH)),
        "cf_norm_g": gain((L, CF_WIDTH)),
        "cf_norm_b": small((L, CF_WIDTH)),
        "w_out_cf": dense((L, CF_WIDTH, D_MODEL), CF_WIDTH),
        "w_mix_out": dense((L, D_MODEL, D_MODEL), D_MODEL),
        "xa_norm": gain((L, D_MODEL)),
        "mem_norm": gain((L, D_MODEL)),
        "xa_wq": dense((L, D_MODEL, X_WIDTH), D_MODEL),
        "xa_wkv": dense((L, D_MODEL, 2 * X_WIDTH), D_MODEL),
        "xa_q_gain": gain((L, X_HEAD_DIM)),
        "xa_k_gain": gain((L, X_HEAD_DIM)),
        "xa_wo": dense((L, X_WIDTH, D_MODEL), X_WIDTH),
        "ffn_norm": gain((L, D_MODEL)),
        "w_route_group": dense((L, D_MODEL, N_GROUPS), D_MODEL),
        "b_route_group": small((L, N_GROUPS)),
        "w_route_expert": dense((L, D_MODEL, N_EXPERTS), D_MODEL),
        "b_route_expert": small((L, N_EXPERTS)),
        "w_expert_gu": dense((L, N_EXPERTS, D_MODEL, 2 * D_EXPERT), D_MODEL),
        "w_expert_down": dense((L, N_EXPERTS, D_EXPERT, D_MODEL), D_EXPERT),
    }


def reference(x, mem, mix_norm, w_in, sc_conv, attn_q_gain, attn_k_gain, attn_sinks,
              w_out_sc, w_out_attn, cf_conv, cf_conv_b, cf_norm_g, cf_norm_b, w_out_cf,
              w_mix_out, xa_norm, mem_norm, xa_wq, xa_wkv, xa_q_gain, xa_k_gain, xa_wo,
              ffn_norm, w_route_group, b_route_group, w_route_expert, b_route_expert,
              w_expert_gu, w_expert_down):
    b, s, _ = x.shape
    for l in range(DEPTH):
        h = rms_norm(x, mix_norm[l])
        (sc_b, sc_c, sc_x, q, k, v, cf_val, cf_gate,
         g_sc, g_attn, g_cf) = jnp.split(h @ w_in[l], SPLIT_POINTS, axis=-1)
        y_sc = (sc_b * causal_depthwise_conv(sc_c * sc_x, sc_conv[l])) @ w_out_sc[l]
        attn = sliding_window_attention(
            q.reshape(b, s, N_Q_HEADS, HEAD_DIM), k.reshape(b, s, N_KV_HEADS, HEAD_DIM),
            v.reshape(b, s, N_KV_HEADS, HEAD_DIM), attn_q_gain[l], attn_k_gain[l], attn_sinks[l])
        y_attn = attn @ w_out_attn[l]
        u = cf_val * jax.nn.sigmoid(cf_gate)
        u = causal_depthwise_conv(u, cf_conv[l]) + cf_conv_b[l]
        u = jax.nn.silu(layer_norm(u, cf_norm_g[l], cf_norm_b[l]))
        y_cf = u @ w_out_cf[l]
        merged = (jax.nn.sigmoid(g_sc) * y_sc + jax.nn.sigmoid(g_attn) * y_attn
                  + jax.nn.sigmoid(g_cf) * y_cf)
        x = x + merged @ w_mix_out[l]
        x = x + memory_cross_attention(rms_norm(x, xa_norm[l]), rms_norm(mem, mem_norm[l]),
                                       xa_wq[l], xa_wkv[l], xa_q_gain[l], xa_k_gain[l], xa_wo[l])
        x = x + hierarchical_moe(rms_norm(x, ffn_norm[l]), w_route_group[l], b_route_group[l],
                                 w_route_expert[l], b_route_expert[l], w_expert_gu[l],
                                 w_expert_down[l])
    return x
```

```python
import functools

import jax
import jax.numpy as jnp
from jax import lax
from jax.experimental import pallas as pl
from jax.experimental.pallas import tpu as pltpu

F32 = jnp.float32
BF16 = jnp.bfloat16
I32 = jnp.int32

EPS = 1e-6
NEG_INF = -1e30
ATTN_BLOCK = 128
TOP_K = 2
LANES = 128
SUBLANES = 8
TOKEN_TILE = 256
EXPERT_TILE = 256
VMEM_LIMIT = 56 * 1024 * 1024


def _const_spec(shape):
    nd = len(shape)
    return pl.BlockSpec(shape, lambda *_: (0,) * nd, pipeline_mode=pl.Buffered(1))


def _tile_spec(tm, width):
    return pl.BlockSpec((1, tm, width), lambda b, s: (b, s, 0))


def _params(n_axes):
    return pltpu.CompilerParams(dimension_semantics=("arbitrary",) * n_axes,
                                vmem_limit_bytes=VMEM_LIMIT)


def _rms(xf, gain):
    return xf * lax.rsqrt(jnp.mean(xf * xf, axis=-1, keepdims=True) + EPS) * gain


def _dot(a, b):
    return jnp.dot(a, b, preferred_element_type=F32)


def _dot_t(a, b):
    return lax.dot_general(a, b, (((1,), (1,)), ((), ())), preferred_element_type=F32)


def _halo_rows(taps):
    return -(-(taps - 1) // SUBLANES) * SUBLANES


def _causal_conv(ext_ref, w_ref, out_ref, tm, taps):
    halo = _halo_rows(taps)
    chans = ext_ref.shape[1]
    rows = min(64, tm)
    groups = {}
    for k in range(taps):
        off = halo - (taps - 1) + k
        groups.setdefault(off % SUBLANES, []).append((off // SUBLANES, k))

    def col_chunk(c, carry):
        cols = pl.ds(pl.multiple_of(c * LANES, LANES), LANES)
        for r0 in range(0, tm, rows):
            acc = None
            for mis, members in groups.items():
                span = SUBLANES * max(a for a, _ in members) + rows
                shifted = ext_ref[pl.ds(r0 + mis, span), cols]
                for a, k in members:
                    term = w_ref[pl.ds(k, 1), cols] * shifted[SUBLANES * a:SUBLANES * a + rows]
                    acc = term if acc is None else acc + term
            out_ref[pl.ds(r0, rows), cols] = acc
        return carry

    lax.fori_loop(0, chans // LANES, col_chunk, 0)
    ext_ref[0:halo, :] = ext_ref[tm:tm + halo, :]


def _sc_kernel(x_ref, g_ref, wsc_ref, cw_ref, wo_ref, wg_ref, out_ref, ext_ref, cv_ref, *, taps):
    tm = x_ref.shape[1]
    sc = cw_ref.shape[1]
    halo = _halo_rows(taps)

    @pl.when(pl.program_id(1) == 0)
    def _():
        ext_ref[0:halo, :] = jnp.zeros((halo, sc), F32)

    h = _rms(x_ref[0], g_ref[...]).astype(BF16)
    p = _dot(h, wsc_ref[...])
    ext_ref[halo:halo + tm, :] = p[:, sc:2 * sc] * p[:, 2 * sc:]
    _causal_conv(ext_ref, cw_ref, cv_ref, tm, taps)
    v = (p[:, :sc] * cv_ref[...]).astype(BF16)
    y = _dot(v, wo_ref[...])
    gate = jax.nn.sigmoid(_dot(h, wg_ref[...]))
    out_ref[0] = (gate * y).astype(BF16)


def _sc_branch(x, gain, wsc, cw, wo, wg):
    b, s, d = x.shape
    taps, sc = cw.shape
    tm = min(TOKEN_TILE, s)
    halo = _halo_rows(taps)
    return pl.pallas_call(
        functools.partial(_sc_kernel, taps=taps),
        grid=(b, s // tm),
        in_specs=[_tile_spec(tm, d), _const_spec(gain.shape), _const_spec(wsc.shape),
                  _const_spec(cw.shape), _const_spec(wo.shape), _const_spec(wg.shape)],
        out_specs=_tile_spec(tm, d),
        out_shape=jax.ShapeDtypeStruct((b, s, d), BF16),
        scratch_shapes=[pltpu.VMEM((halo + tm, sc), F32), pltpu.VMEM((tm, sc), F32)],
        compiler_params=_params(2),
        name="mixer_gated_conv",
    )(x, gain, wsc, cw, wo, wg)


def _swa_kernel(sink_ref, x_ref, g_ref, wqkv_ref, gq_ref, gk_ref, prev_ref, wo_ref, wg_ref,
                out_ref, kbuf, vbuf, attn_ref, *, n_q, n_kv, hd):
    tm = x_ref.shape[1]
    blk = ATTN_BLOCK
    grp = n_q // n_kv
    qw = n_q * hd
    kw = n_kv * LANES
    s_idx = pl.program_id(1)

    @pl.when(s_idx == 0)
    def _():
        kbuf[0:blk, :] = jnp.zeros((blk, kw), BF16)
        vbuf[0:blk, :] = jnp.zeros((blk, kw), BF16)

    h = _rms(x_ref[0], g_ref[...]).astype(BF16)
    qkv = _dot(h, wqkv_ref[...])
    for g in range(n_kv):
        ks = qkv[:, qw + g * LANES:qw + (g + 1) * LANES]
        ms = jnp.sum(ks * ks, axis=-1, keepdims=True) * (1.0 / LANES)
        kbuf[blk:blk + tm, g * LANES:(g + 1) * LANES] = (
            ks * lax.rsqrt(ms + EPS) * gk_ref[...]).astype(BF16)
    vbuf[blk:blk + tm, :] = qkv[:, qw + kw:].astype(BF16)

    lane = lax.broadcasted_iota(I32, (blk, LANES), 1)
    qi = lax.broadcasted_iota(I32, (blk, 2 * blk), 0)
    kj = lax.broadcasted_iota(I32, (blk, 2 * blk), 1)
    for jb in range(tm // blk):
        r0 = jb * blk
        first = jnp.logical_and(s_idx == 0, jb == 0)
        lo = jnp.where(first, blk, 0)
        valid = (kj >= jnp.maximum(qi + 1, lo)) & (kj <= qi + blk)
        for g in range(n_kv):
            kk = kbuf[r0:r0 + 2 * blk, g * LANES:(g + 1) * LANES]
            vv = vbuf[r0:r0 + 2 * blk, g * LANES:(g + 1) * LANES]
            qn = []
            for r in range(grp):
                head = g * grp + r
                slab, half = head // 2, head % 2
                qs = qkv[r0:r0 + blk, slab * LANES:(slab + 1) * LANES]
                qm = jnp.where(lane >= hd if half else lane < hd, qs, 0.0)
                ms = jnp.sum(qm * qm, axis=-1, keepdims=True) * (1.0 / hd)
                qn.append((qm * lax.rsqrt(ms + EPS) * gq_ref[...]).astype(BF16))
            scores = _dot_t(jnp.concatenate(qn, axis=0), kk)
            probs, dens = [], []
            for r in range(grp):
                sink = sink_ref[g * grp + r]
                sc = jnp.where(valid, scores[r * blk:(r + 1) * blk], NEG_INF)
                m = jnp.maximum(jnp.max(sc, axis=-1, keepdims=True), sink)
                p = jnp.exp(sc - m)
                dens.append(jnp.sum(p, axis=-1, keepdims=True) + jnp.exp(sink - m))
                probs.append(p.astype(BF16))
            o = _dot(jnp.concatenate(probs, axis=0), vv)
            outs = [o[r * blk:(r + 1) * blk] / dens[r] for r in range(grp)]
            for pr in range(grp // 2):
                slab = (g * grp) // 2 + pr
                attn_ref[r0:r0 + blk, slab * LANES:(slab + 1) * LANES] = jnp.where(
                    lane < hd, outs[2 * pr], outs[2 * pr + 1])

    kbuf[0:blk, :] = kbuf[tm:tm + blk, :]
    vbuf[0:blk, :] = vbuf[tm:tm + blk, :]
    y = _dot(attn_ref[...].astype(BF16), wo_ref[...])
    gate = jax.nn.sigmoid(_dot(h, wg_ref[...]))
    out_ref[0] = (prev_ref[0].astype(F32) + gate * y).astype(BF16)


def _swa_branch(x, gain, wqkv, gq, gk, sinks, prev, wo, wg, *, n_q, n_kv, hd):
    b, s, d = x.shape
    tm = min(TOKEN_TILE, s)
    kw = n_kv * LANES
    tile = lambda bb, ss, *_: (bb, ss, 0)
    const = lambda shape: pl.BlockSpec(shape, lambda *_: (0,) * len(shape),
                                       pipeline_mode=pl.Buffered(1))
    grid_spec = pltpu.PrefetchScalarGridSpec(
        num_scalar_prefetch=1,
        grid=(b, s // tm),
        in_specs=[pl.BlockSpec((1, tm, d), tile), const(gain.shape), const(wqkv.shape),
                  const(gq.shape), const(gk.shape), pl.BlockSpec((1, tm, d), tile),
                  const(wo.shape), const(wg.shape)],
        out_specs=pl.BlockSpec((1, tm, d), tile),
        scratch_shapes=[pltpu.VMEM((ATTN_BLOCK + tm, kw), BF16),
                        pltpu.VMEM((ATTN_BLOCK + tm, kw), BF16),
                        pltpu.VMEM((tm, n_q * hd), F32)],
    )
    return pl.pallas_call(
        functools.partial(_swa_kernel, n_q=n_q, n_kv=n_kv, hd=hd),
        grid_spec=grid_spec,
        out_shape=jax.ShapeDtypeStruct((b, s, d), BF16),
        compiler_params=_params(2),
        name="mixer_swa",
    )(sinks, x, gain, wqkv, gq, gk, prev, wo, wg)


def _cf_kernel(x_ref, g_ref, wcf_ref, cw_ref, cb_ref, lng_ref, lnb_ref, prev_ref, wo_ref, wg_ref,
               out_ref, ext_ref, cv_ref, *, taps):
    tm = x_ref.shape[1]
    cf = cw_ref.shape[1]
    halo = _halo_rows(taps)

    @pl.when(pl.program_id(1) == 0)
    def _():
        ext_ref[0:halo, :] = jnp.zeros((halo, cf), F32)

    h = _rms(x_ref[0], g_ref[...]).astype(BF16)
    p = _dot(h, wcf_ref[...])
    ext_ref[halo:halo + tm, :] = p[:, :cf] * jax.nn.sigmoid(p[:, cf:])
    _causal_conv(ext_ref, cw_ref, cv_ref, tm, taps)
    u = cv_ref[...] + cb_ref[...]
    xc = u - jnp.mean(u, axis=-1, keepdims=True)
    ln = xc * lax.rsqrt(jnp.mean(xc * xc, axis=-1, keepdims=True) + EPS) * lng_ref[...] + lnb_ref[...]
    act = (ln * jax.nn.sigmoid(ln)).astype(BF16)
    y = _dot(act, wo_ref[...])
    gate = jax.nn.sigmoid(_dot(h, wg_ref[...]))
    out_ref[0] = (prev_ref[0].astype(F32) + gate * y).astype(BF16)


def _cf_branch(x, gain, wcf, cw, cb, lng, lnb, prev, wo, wg):
    b, s, d = x.shape
    taps, cf = cw.shape
    tm = min(TOKEN_TILE, s)
    halo = _halo_rows(taps)
    return pl.pallas_call(
        functools.partial(_cf_kernel, taps=taps),
        grid=(b, s // tm),
        in_specs=[_tile_spec(tm, d), _const_spec(gain.shape), _const_spec(wcf.shape),
                  _const_spec(cw.shape), _const_spec(cb.shape), _const_spec(lng.shape),
                  _const_spec(lnb.shape), _tile_spec(tm, d), _const_spec(wo.shape),
                  _const_spec(wg.shape)],
        out_specs=_tile_spec(tm, d),
        out_shape=jax.ShapeDtypeStruct((b, s, d), BF16),
        scratch_shapes=[pltpu.VMEM((halo + tm, cf), F32), pltpu.VMEM((tm, cf), F32)],
        compiler_params=_params(2),
        name="mixer_conformer_conv",
    )(x, gain, wcf, cw, cb, lng, lnb, prev, wo, wg)


def _memkv_kernel(mem_ref, g_ref, wkv_ref, gk_ref, k_ref, v_ref, *, n_heads, hd):
    xw = n_heads * hd
    mn = _rms(mem_ref[0], g_ref[...]).astype(BF16)
    kv = _dot(mn, wkv_ref[...])
    for hh in range(n_heads):
        k_ref[0, :, hh * hd:(hh + 1) * hd] = _rms(kv[:, hh * hd:(hh + 1) * hd], gk_ref[...]).astype(BF16)
    v_ref[0] = kv[:, xw:].astype(BF16)


def _mem_kv(mem, gain, wkv, gk, *, n_heads, hd):
    b, m, d = mem.shape
    xw = n_heads * hd
    spec_m = pl.BlockSpec((1, m, d), lambda i: (i, 0, 0))
    spec_o = pl.BlockSpec((1, m, xw), lambda i: (i, 0, 0))
    return pl.pallas_call(
        functools.partial(_memkv_kernel, n_heads=n_heads, hd=hd),
        grid=(b,),
        in_specs=[spec_m, _const_spec(gain.shape), _const_spec(wkv.shape), _const_spec(gk.shape)],
        out_specs=[spec_o, spec_o],
        out_shape=[jax.ShapeDtypeStruct((b, m, xw), BF16)] * 2,
        compiler_params=_params(1),
        name="memory_kv",
    )(mem, gain, wkv, gk)


def _xattn_router_kernel(x_ref, mg_ref, wmix_ref, gx_ref, wq_ref, gq_ref, k_ref, v_ref, wo_ref,
                         gf_ref, wr_ref, br_ref, x_out, h_out, route_out, cnt_out, carry_ref,
                         *, n_heads, hd, n_groups, per_group):
    tm = x_ref.shape[1]

    @pl.when(jnp.logical_and(pl.program_id(0) == 0, pl.program_id(1) == 0))
    def _():
        carry_ref[...] = jnp.zeros_like(carry_ref)

    x1 = x_ref[0] + _dot(mg_ref[0], wmix_ref[...])
    q = _dot(_rms(x1, gx_ref[...]).astype(BF16), wq_ref[...])
    outs = []
    for hh in range(n_heads):
        qn = (_rms(q[:, hh * hd:(hh + 1) * hd], gq_ref[...])).astype(BF16)
        sc = _dot_t(qn, k_ref[0, :, hh * hd:(hh + 1) * hd])
        p = jnp.exp(sc - jnp.max(sc, axis=-1, keepdims=True))
        den = jnp.sum(p, axis=-1, keepdims=True)
        outs.append(_dot(p.astype(BF16), v_ref[0, :, hh * hd:(hh + 1) * hd]) / den)
    x2 = x1 + _dot(jnp.concatenate(outs, axis=-1).astype(BF16), wo_ref[...])
    x_out[0] = x2
    hf = _rms(x2, gf_ref[...])
    h_out[0] = hf
    lg = _dot(hf.astype(BF16), wr_ref[...]) + br_ref[...]
    lane = lax.broadcasted_iota(I32, (tm, LANES), 1).astype(F32)

    def first_lane(mask):
        return jnp.min(jnp.where(mask, lane, float(LANES)), axis=-1, keepdims=True)

    gmask = lane < n_groups
    gl = jnp.where(gmask, lg, NEG_INF)
    ge = jnp.where(gmask, jnp.exp(gl - jnp.max(gl, axis=-1, keepdims=True)), 0.0)
    pg = ge / jnp.sum(ge, axis=-1, keepdims=True)
    pg_top = jnp.max(pg, axis=-1, keepdims=True)
    g_idx = first_lane((pg == pg_top) & gmask)
    lo = n_groups + g_idx * per_group
    emask = (lane >= lo) & (lane < lo + per_group)
    el = jnp.where(emask, lg, NEG_INF)
    ee = jnp.where(emask, jnp.exp(el - jnp.max(el, axis=-1, keepdims=True)), 0.0)
    pe = ee / jnp.sum(ee, axis=-1, keepdims=True)
    pe = jnp.where(emask, pe, -1.0)
    p1 = jnp.max(pe, axis=-1, keepdims=True)
    i1 = first_lane(pe == p1)
    pe2 = jnp.where(lane == i1, -1.0, pe)
    p2 = jnp.max(pe2, axis=-1, keepdims=True)
    i2 = first_lane((pe2 == p2) & emask & (lane != i1))
    den = p1 + p2
    w1 = pg_top * (p1 / den)
    w2 = pg_top * (p2 / den)
    hit1, hit2 = lane == i1, lane == i2
    assign = jnp.where(hit1 | hit2, 1.0, 0.0)
    ri = lax.broadcasted_iota(I32, (tm, tm), 0)
    rj = lax.broadcasted_iota(I32, (tm, tm), 1)
    lower = jnp.where(rj < ri, 1.0, 0.0).astype(BF16)
    before = _dot(lower, assign.astype(BF16)) + carry_ref[0:1, :]
    rank1 = jnp.sum(jnp.where(hit1, before, 0.0), axis=-1, keepdims=True)
    rank2 = jnp.sum(jnp.where(hit2, before, 0.0), axis=-1, keepdims=True)
    carry_ref[...] = carry_ref[...] + jnp.sum(assign, axis=0, keepdims=True)
    cnt_out[...] = carry_ref[...]
    fields = [i1 - n_groups, i2 - n_groups, rank1, rank2, w1, w2]
    packed = jnp.zeros((tm, LANES), F32)
    for idx, val in enumerate(fields):
        packed = jnp.where(lane == idx, val, packed)
    route_out[0] = packed


def _xattn_router(x, merged, wmix, gx, wq, gq, kx, vx, wo, gf, wr, br, *, n_heads, hd, n_groups,
                  per_group):
    b, s, d = x.shape
    m, xw = kx.shape[1:]
    tm = min(TOKEN_TILE, s)
    kv_spec = pl.BlockSpec((1, m, xw), lambda bb, ss: (bb, 0, 0))
    return pl.pallas_call(
        functools.partial(_xattn_router_kernel, n_heads=n_heads, hd=hd, n_groups=n_groups,
                          per_group=per_group),
        grid=(b, s // tm),
        in_specs=[_tile_spec(tm, d), _tile_spec(tm, d), _const_spec(wmix.shape),
                  _const_spec(gx.shape), _const_spec(wq.shape), _const_spec(gq.shape),
                  kv_spec, kv_spec, _const_spec(wo.shape), _const_spec(gf.shape),
                  _const_spec(wr.shape), _const_spec(br.shape)],
        out_specs=[_tile_spec(tm, d), _tile_spec(tm, d), _tile_spec(tm, LANES),
                   pl.BlockSpec((SUBLANES, LANES), lambda bb, ss: (0, 0))],
        out_shape=[jax.ShapeDtypeStruct((b, s, d), F32), jax.ShapeDtypeStruct((b, s, d), F32),
                   jax.ShapeDtypeStruct((b, s, LANES), F32),
                   jax.ShapeDtypeStruct((SUBLANES, LANES), F32)],
        scratch_shapes=[pltpu.VMEM((SUBLANES, LANES), F32)],
        compiler_params=_params(2),
        name="mixout_xattn_router",
    )(x, merged, wmix, gx, wq, gq, kx, vx, wo, gf, wr, br)


def _dispatch_kernel(seg_end_ref, cnt_ref, used_ref, pos_ref, h_ref, xs_ref, zero_ref, sems, *,
                     n_experts, tile_rows):
    step = pl.program_id(0)
    tokens = pos_ref.shape[0] // TOP_K
    max_tiles = xs_ref.shape[0] // tile_rows

    def fill_copy(row_start):
        start = pl.multiple_of(row_start, tile_rows)
        return pltpu.make_async_copy(zero_ref, xs_ref.at[pl.ds(start, tile_rows)], sems.at[0])

    @pl.when(step == 0)
    def _():
        zero_ref[...] = jnp.zeros_like(zero_ref)
        for e in range(n_experts):
            @pl.when(cnt_ref[e] > 0)
            def _():
                fill_copy(seg_end_ref[e] - tile_rows).start()

        def start_tail(i, c):
            fill_copy(i * tile_rows).start()
            return c

        def wait_tail(i, c):
            fill_copy(i * tile_rows).wait()
            return c

        lax.fori_loop(used_ref[0], max_tiles, start_tail, 0)
        for e in range(n_experts):
            @pl.when(cnt_ref[e] > 0)
            def _():
                fill_copy(seg_end_ref[e] - tile_rows).wait()
        lax.fori_loop(used_ref[0], max_tiles, wait_tail, 0)

    def row_copy(r, k):
        return pltpu.make_async_copy(h_ref.at[pl.ds(step * tokens + r, 1)],
                                     xs_ref.at[pl.ds(pos_ref[TOP_K * r + k], 1)], sems.at[1])

    def start(r, c):
        for k in range(TOP_K):
            row_copy(r, k).start()
        return c

    def wait(r, c):
        for k in range(TOP_K):
            row_copy(r, k).wait()
        return c

    lax.fori_loop(0, tokens, start, 0)
    lax.fori_loop(0, tokens, wait, 0)


def _dispatch(seg_end, cnt, n_used, pos, h, *, rows_out, tile_rows):
    t, d = h.shape
    tokens = min(TOKEN_TILE, t)
    grid_spec = pltpu.PrefetchScalarGridSpec(
        num_scalar_prefetch=3,
        grid=(t // tokens,),
        in_specs=[pl.BlockSpec((tokens * TOP_K,), lambda i, *_: (i,), memory_space=pltpu.SMEM),
                  pl.BlockSpec(memory_space=pl.ANY)],
        out_specs=pl.BlockSpec(memory_space=pl.ANY),
        scratch_shapes=[pltpu.VMEM((tile_rows, d), F32), pltpu.SemaphoreType.DMA((2,))],
    )
    return pl.pallas_call(
        functools.partial(_dispatch_kernel, n_experts=cnt.shape[0], tile_rows=tile_rows),
        grid_spec=grid_spec,
        out_shape=jax.ShapeDtypeStruct((rows_out, d), F32),
        compiler_params=_params(1),
        name="moe_dispatch",
    )(seg_end, cnt, n_used, pos, h)


def _expert_kernel(exp_ref, used_ref, xs_ref, wgu_ref, wdn_ref, ys_ref):
    de = wdn_ref.shape[1]
    live = pl.program_id(0) < used_ref[0]

    @pl.when(live)
    def _():
        gu = _dot(xs_ref[...].astype(BF16), wgu_ref[0])
        act = (jax.nn.silu(gu[:, :de]) * gu[:, de:]).astype(BF16)
        ys_ref[...] = _dot(act, wdn_ref[0])

    @pl.when(jnp.logical_not(live))
    def _():
        ys_ref[...] = jnp.zeros_like(ys_ref)


def _experts(tile_exp, n_used, xs, wgu, wdn, *, tile_rows):
    rows, d = xs.shape
    de = wdn.shape[1]
    grid_spec = pltpu.PrefetchScalarGridSpec(
        num_scalar_prefetch=2,
        grid=(rows // tile_rows,),
        in_specs=[pl.BlockSpec((tile_rows, d), lambda i, te, nu: (i, 0)),
                  pl.BlockSpec((1, d, 2 * de), lambda i, te, nu: (te[i], 0, 0)),
                  pl.BlockSpec((1, de, d), lambda i, te, nu: (te[i], 0, 0))],
        out_specs=pl.BlockSpec((tile_rows, d), lambda i, te, nu: (i, 0)),
    )
    return pl.pallas_call(
        _expert_kernel,
        grid_spec=grid_spec,
        out_shape=jax.ShapeDtypeStruct((rows, d), F32),
        compiler_params=_params(1),
        name="moe_experts",
    )(tile_exp, n_used, xs, wgu, wdn)


def _combine_kernel(pos_ref, x_ref, route_ref, ys_ref, out_ref, buf, sems):
    tokens = x_ref.shape[0]

    def row_copy(r, k):
        return pltpu.make_async_copy(ys_ref.at[pl.ds(pos_ref[TOP_K * r + k], 1)],
                                     buf.at[k, pl.ds(r, 1)], sems.at[k])

    def start(r, c):
        for k in range(TOP_K):
            row_copy(r, k).start()
        return c

    def wait(r, c):
        for k in range(TOP_K):
            row_copy(r, k).wait()
        return c

    lax.fori_loop(0, tokens, start, 0)
    lax.fori_loop(0, tokens, wait, 0)
    acc = x_ref[...]
    for k in range(TOP_K):
        acc = acc + route_ref[:, 4 + k:5 + k] * buf[k]
    out_ref[...] = acc


def _combine(pos, x, route, ys):
    t, d = x.shape
    tokens = min(TOKEN_TILE, t)
    return pl.pallas_call(
        _combine_kernel,
        grid=(t // tokens,),
        in_specs=[pl.BlockSpec((tokens * TOP_K,), lambda i: (i,), memory_space=pltpu.SMEM),
                  pl.BlockSpec((tokens, d), lambda i: (i, 0)),
                  pl.BlockSpec((tokens, LANES), lambda i: (i, 0)),
                  pl.BlockSpec(memory_space=pl.ANY)],
        out_specs=pl.BlockSpec((tokens, d), lambda i: (i, 0)),
        out_shape=jax.ShapeDtypeStruct((t, d), F32),
        scratch_shapes=[pltpu.VMEM((TOP_K, tokens, d), F32), pltpu.SemaphoreType.DMA((TOP_K,))],
        compiler_params=_params(1),
        name="moe_combine",
    )(pos, x, route, ys)


def _dup_heads(w, n_heads, hd):
    d = w.shape[0]
    w3 = w.reshape(d, n_heads, hd)
    return jnp.concatenate([w3, w3], axis=-1).reshape(d, n_heads * 2 * hd)


def _row(v):
    return v.reshape(1, -1).astype(F32)


def kernel(x, mem, mix_norm, w_in, sc_conv, attn_q_gain, attn_k_gain, attn_sinks, w_out_sc, w_out_attn, cf_conv, cf_conv_b, cf_norm_g, cf_norm_b, w_out_cf, w_mix_out, xa_norm, mem_norm, xa_wq, xa_wkv, xa_q_gain, xa_k_gain, xa_wo, ffn_norm, w_route_group, b_route_group, w_route_expert, b_route_expert, w_expert_gu, w_expert_down):
    b, s, d = x.shape
    depth = w_in.shape[0]
    sc = sc_conv.shape[-1]
    cf = cf_conv.shape[-1]
    hd = attn_q_gain.shape[-1]
    n_q = attn_sinks.shape[-1]
    qw = n_q * hd
    kvw = (w_in.shape[-1] - 3 * sc - qw - 2 * cf - 3 * d) // 2
    n_kv = kvw // hd
    x_hd = xa_q_gain.shape[-1]
    x_heads = xa_wq.shape[-1] // x_hd
    n_groups = w_route_group.shape[-1]
    n_experts = w_route_expert.shape[-1]
    per_group = n_experts // n_groups
    assert 2 * hd == LANES and n_q % (2 * n_kv) == 0 and x_hd == LANES
    assert n_groups + n_experts <= LANES and s % ATTN_BLOCK == 0
    t = b * s
    tile_rows = EXPERT_TILE
    max_tiles = (t * TOP_K) // tile_rows + n_experts

    o_q = 3 * sc
    o_k = o_q + qw
    o_v = o_k + kvw
    o_cf = o_v + kvw
    o_g = o_cf + 2 * cf
    gq = jnp.concatenate([attn_q_gain, attn_q_gain], axis=-1) * (hd ** -0.5)
    gk = jnp.concatenate([attn_k_gain, attn_k_gain], axis=-1)

    for l in range(depth):
        wl = w_in[l].astype(BF16)
        wqkv = jnp.concatenate([wl[:, o_q:o_k], _dup_heads(wl[:, o_k:o_v], n_kv, hd),
                                _dup_heads(wl[:, o_v:o_cf], n_kv, hd)], axis=-1)
        gmix = _row(mix_norm[l])
        m1 = _sc_branch(x, gmix, wl[:, :o_q], sc_conv[l], w_out_sc[l].astype(BF16),
                        wl[:, o_g:o_g + d])
        m2 = _swa_branch(x, gmix, wqkv, _row(gq[l]), _row(gk[l]), attn_sinks[l], m1,
                         w_out_attn[l].astype(BF16), wl[:, o_g + d:o_g + 2 * d],
                         n_q=n_q, n_kv=n_kv, hd=hd)
        m3 = _cf_branch(x, gmix, wl[:, o_cf:o_g], cf_conv[l], _row(cf_conv_b[l]),
                        _row(cf_norm_g[l]), _row(cf_norm_b[l]), m2, w_out_cf[l].astype(BF16),
                        wl[:, o_g + 2 * d:o_g + 3 * d])
        kx, vx = _mem_kv(mem, _row(mem_norm[l]), xa_wkv[l].astype(BF16), _row(xa_k_gain[l]),
                         n_heads=x_heads, hd=x_hd)
        wr = jnp.concatenate(
            [w_route_group[l], w_route_expert[l],
             jnp.zeros((d, LANES - n_groups - n_experts), F32)], axis=-1).astype(BF16)
        br = jnp.concatenate(
            [b_route_group[l], b_route_expert[l],
             jnp.zeros((LANES - n_groups - n_experts,), F32)]).reshape(1, LANES)
        x2, hf, route, counts = _xattn_router(
            x, m3, w_mix_out[l].astype(BF16), _row(xa_norm[l]), xa_wq[l].astype(BF16),
            _row(xa_q_gain[l] * (x_hd ** -0.5)), kx, vx, xa_wo[l].astype(BF16),
            _row(ffn_norm[l]), wr, br, n_heads=x_heads, hd=x_hd, n_groups=n_groups,
            per_group=per_group)
        cnt = counts[0, n_groups:n_groups + n_experts].astype(I32)
        n_tile = (cnt + tile_rows - 1) // tile_rows
        tile_end = jnp.cumsum(n_tile)
        seg_start = (tile_end - n_tile) * tile_rows
        route2 = route.reshape(t, LANES)
        eid = route2[:, 0:TOP_K].astype(I32)
        rank = route2[:, TOP_K:2 * TOP_K].astype(I32)
        pos = (seg_start[eid] + rank).reshape(t * TOP_K)
        n_used = tile_end[-1:].astype(I32)
        tile_live = jnp.minimum(jnp.arange(max_tiles, dtype=I32), n_used - 1)
        tile_exp = jnp.minimum(jnp.searchsorted(tile_end, tile_live, side="right"),
                               n_experts - 1).astype(I32)
        xs = _dispatch(tile_end * tile_rows, cnt, n_used, pos, hf.reshape(t, d),
                       rows_out=max_tiles * tile_rows, tile_rows=tile_rows)
        ys = _experts(tile_exp, n_used, xs, w_expert_gu[l].astype(BF16),
                      w_expert_down[l].astype(BF16), tile_rows=tile_rows)
        x = _combine(pos, x2.reshape(t, d), route2, ys).reshape(b, s, d)
    return x
```

```python
import functools

import jax
import jax.numpy as jnp
from jax import lax
from jax.experimental import pallas as pl
from jax.experimental.pallas import tpu as pltpu

F32 = jnp.float32
BF16 = jnp.bfloat16
I32 = jnp.int32
U32 = jnp.uint32

EPS = 1e-6
NEG_INF = -1e30
ATTN_BLOCK = 128
TOP_K = 2
LANES = 128
SUBLANES = 8
TOKEN_TILE = 256
EXPERT_TILE = 256
DMA_UNROLL = 8
VMEM_LIMIT = 56 * 1024 * 1024


def _const_spec(shape):
    nd = len(shape)
    return pl.BlockSpec(shape, lambda *_: (0,) * nd, pipeline_mode=pl.Buffered(1))


def _tile_spec(tm, width):
    return pl.BlockSpec((1, tm, width), lambda b, s: (b, s, 0))


def _params(n_axes):
    return pltpu.CompilerParams(dimension_semantics=("arbitrary",) * n_axes,
                                vmem_limit_bytes=VMEM_LIMIT)


def _rms(xf, gain):
    return xf * lax.rsqrt(jnp.mean(xf * xf, axis=-1, keepdims=True) + EPS) * gain


def _dot(a, b):
    return jnp.dot(a, b, preferred_element_type=F32)


def _dot_t(a, b):
    return lax.dot_general(a, b, (((1,), (1,)), ((), ())), preferred_element_type=F32)


def _pack_halves(v):
    w = v.shape[1] // 2
    hi = pltpu.bitcast(v[:, :w].astype(BF16).astype(F32), U32)
    lo = pltpu.bitcast(v[:, w:].astype(BF16).astype(F32), U32)
    return hi | (lo >> 16)


def _unpack_halves(u):
    return pltpu.bitcast(u & jnp.uint32(0xFFFF0000), F32), pltpu.bitcast(u << 16, F32)


def _halo_rows(taps):
    return -(-(taps - 1) // SUBLANES) * SUBLANES


def _conv_cols(ext_ref, w_ref, out_ref, tm, taps, cols):
    halo = _halo_rows(taps)
    rows = min(64, tm)
    groups = {}
    for k in range(taps):
        off = halo - (taps - 1) + k
        groups.setdefault(off % SUBLANES, []).append((off // SUBLANES, k))
    for r0 in range(0, tm, rows):
        total = None
        for mis in sorted(groups):
            members = groups[mis]
            span = SUBLANES * max(a for a, _ in members) + rows
            shifted = ext_ref[pl.ds(r0 + mis, span), cols]
            part = None
            for a, k in members:
                term = w_ref[pl.ds(k, 1), cols] * shifted[SUBLANES * a:SUBLANES * a + rows]
                part = term if part is None else part + term
            total = part if total is None else total + part
        out_ref[pl.ds(r0, rows), cols] = total


def _conv_carry(ext_ref, tm, taps):
    halo = _halo_rows(taps)
    ext_ref[0:halo, :] = ext_ref[tm:tm + halo, :]


def _sc_kernel(x_ref, g_ref, wsc_ref, cw_ref, wo_ref, wg_ref, out_ref, ext_ref, cv_ref, *, taps):
    tm = x_ref.shape[1]
    sc = cw_ref.shape[1]
    halo = _halo_rows(taps)

    @pl.when(pl.program_id(1) == 0)
    def _():
        ext_ref[0:halo, :] = jnp.zeros((halo, sc), F32)

    h = _rms(x_ref[0], g_ref[...]).astype(BF16)
    p = _dot(h, wsc_ref[...])
    ext_ref[halo:halo + tm, :] = p[:, sc:2 * sc] * p[:, 2 * sc:]

    def col_chunk(c, carry):
        _conv_cols(ext_ref, cw_ref, cv_ref, tm, taps, pl.ds(pl.multiple_of(c * LANES, LANES), LANES))
        return carry

    lax.fori_loop(0, sc // LANES, col_chunk, 0)
    _conv_carry(ext_ref, tm, taps)
    v = (p[:, :sc] * cv_ref[...]).astype(BF16)
    y = _dot(v, wo_ref[...])
    gate = jax.nn.sigmoid(_dot(h, wg_ref[...]))
    out_ref[0] = (gate * y).astype(BF16)


def _sc_branch(x, gain, wsc, cw, wo, wg):
    b, s, d = x.shape
    taps, sc = cw.shape
    tm = min(TOKEN_TILE, s)
    halo = _halo_rows(taps)
    return pl.pallas_call(
        functools.partial(_sc_kernel, taps=taps),
        grid=(b, s // tm),
        in_specs=[_tile_spec(tm, d), _const_spec(gain.shape), _const_spec(wsc.shape),
                  _const_spec(cw.shape), _const_spec(wo.shape), _const_spec(wg.shape)],
        out_specs=_tile_spec(tm, d),
        out_shape=jax.ShapeDtypeStruct((b, s, d), BF16),
        scratch_shapes=[pltpu.VMEM((halo + tm, sc), F32), pltpu.VMEM((tm, sc), F32)],
        compiler_params=_params(2),
        name="mixer_gated_conv",
    )(x, gain, wsc, cw, wo, wg)


def _swa_kernel(sink_ref, x_ref, g_ref, wqkv_ref, gq_ref, gk_ref, prev_ref, wo_ref, wg_ref,
                out_ref, kbuf, vbuf, attn_ref, *, n_q, n_kv, hd):
    tm = x_ref.shape[1]
    blk = ATTN_BLOCK
    grp = n_q // n_kv
    qw = n_q * hd
    kw = n_kv * LANES
    s_idx = pl.program_id(1)

    @pl.when(s_idx == 0)
    def _():
        kbuf[0:blk, :] = jnp.zeros((blk, kw), BF16)
        vbuf[0:blk, :] = jnp.zeros((blk, kw), BF16)

    h = _rms(x_ref[0], g_ref[...]).astype(BF16)
    qkv = _dot(h, wqkv_ref[...])
    for g in range(n_kv):
        ks = qkv[:, qw + g * LANES:qw + (g + 1) * LANES]
        ms = jnp.sum(ks * ks, axis=-1, keepdims=True) * (1.0 / LANES)
        kbuf[blk:blk + tm, g * LANES:(g + 1) * LANES] = (
            ks * lax.rsqrt(ms + EPS) * gk_ref[...]).astype(BF16)
    vbuf[blk:blk + tm, :] = qkv[:, qw + kw:].astype(BF16)

    lane = lax.broadcasted_iota(I32, (blk, LANES), 1)
    qi = lax.broadcasted_iota(I32, (blk, 2 * blk), 0)
    kj = lax.broadcasted_iota(I32, (blk, 2 * blk), 1)
    for jb in range(tm // blk):
        r0 = jb * blk
        first = jnp.logical_and(s_idx == 0, jb == 0)
        lo = jnp.where(first, blk, 0)
        valid = (kj >= jnp.maximum(qi + 1, lo)) & (kj <= qi + blk)
        for g in range(n_kv):
            kk = kbuf[r0:r0 + 2 * blk, g * LANES:(g + 1) * LANES]
            vv = vbuf[r0:r0 + 2 * blk, g * LANES:(g + 1) * LANES]
            qn = []
            for r in range(grp):
                head = g * grp + r
                slab, half = head // 2, head % 2
                qs = qkv[r0:r0 + blk, slab * LANES:(slab + 1) * LANES]
                qm = jnp.where(lane >= hd if half else lane < hd, qs, 0.0)
                ms = jnp.sum(qm * qm, axis=-1, keepdims=True) * (1.0 / hd)
                qn.append((qm * lax.rsqrt(ms + EPS) * gq_ref[...]).astype(BF16))
            scores = _dot_t(jnp.concatenate(qn, axis=0), kk)
            probs, dens = [], []
            for r in range(grp):
                sink = sink_ref[g * grp + r]
                sc = jnp.where(valid, scores[r * blk:(r + 1) * blk], NEG_INF)
                m = jnp.maximum(jnp.max(sc, axis=-1, keepdims=True), sink)
                p = jnp.exp(sc - m)
                dens.append(jnp.sum(p, axis=-1, keepdims=True) + jnp.exp(sink - m))
                probs.append(p.astype(BF16))
            o = _dot(jnp.concatenate(probs, axis=0), vv)
            outs = [o[r * blk:(r + 1) * blk] / dens[r] for r in range(grp)]
            for pr in range(grp // 2):
                slab = (g * grp) // 2 + pr
                attn_ref[r0:r0 + blk, slab * LANES:(slab + 1) * LANES] = jnp.where(
                    lane < hd, outs[2 * pr], outs[2 * pr + 1])

    kbuf[0:blk, :] = kbuf[tm:tm + blk, :]
    vbuf[0:blk, :] = vbuf[tm:tm + blk, :]
    y = _dot(attn_ref[...].astype(BF16), wo_ref[...])
    gate = jax.nn.sigmoid(_dot(h, wg_ref[...]))
    out_ref[0] = (prev_ref[0].astype(F32) + gate * y).astype(BF16)


def _swa_branch(x, gain, wqkv, gq, gk, sinks, prev, wo, wg, *, n_q, n_kv, hd):
    b, s, d = x.shape
    tm = min(TOKEN_TILE, s)
    kw = n_kv * LANES
    tile = lambda bb, ss, *_: (bb, ss, 0)
    const = lambda shape: pl.BlockSpec(shape, lambda *_: (0,) * len(shape),
                                       pipeline_mode=pl.Buffered(1))
    grid_spec = pltpu.PrefetchScalarGridSpec(
        num_scalar_prefetch=1,
        grid=(b, s // tm),
        in_specs=[pl.BlockSpec((1, tm, d), tile), const(gain.shape), const(wqkv.shape),
                  const(gq.shape), const(gk.shape), pl.BlockSpec((1, tm, d), tile),
                  const(wo.shape), const(wg.shape)],
        out_specs=pl.BlockSpec((1, tm, d), tile),
        scratch_shapes=[pltpu.VMEM((ATTN_BLOCK + tm, kw), BF16),
                        pltpu.VMEM((ATTN_BLOCK + tm, kw), BF16),
                        pltpu.VMEM((tm, n_q * hd), F32)],
    )
    return pl.pallas_call(
        functools.partial(_swa_kernel, n_q=n_q, n_kv=n_kv, hd=hd),
        grid_spec=grid_spec,
        out_shape=jax.ShapeDtypeStruct((b, s, d), BF16),
        compiler_params=_params(2),
        name="mixer_swa",
    )(sinks, x, gain, wqkv, gq, gk, prev, wo, wg)


def _cf_kernel(x_ref, g_ref, wcf_ref, cw_ref, cb_ref, lng_ref, lnb_ref, prev_ref, wo_ref, wg_ref,
               out_ref, ext_ref, cv_ref, h_ref, gate_ref, *, taps):
    tm, d = x_ref.shape[1:]
    cf = cw_ref.shape[1]
    halo = _halo_rows(taps)

    @pl.when(pl.program_id(1) == 0)
    def _():
        ext_ref[0:halo, :] = jnp.zeros((halo, cf), F32)

    h_ref[...] = _rms(x_ref[0], g_ref[...]).astype(BF16)
    p = _dot(h_ref[...], wcf_ref[...])
    ext_ref[halo:halo + tm, :] = p[:, :cf] * jax.nn.sigmoid(p[:, cf:])
    n_chunks = cf // LANES
    gcols = d // n_chunks
    for c in range(n_chunks):
        _conv_cols(ext_ref, cw_ref, cv_ref, tm, taps, slice(c * LANES, (c + 1) * LANES))
        gate_ref[:, c * gcols:(c + 1) * gcols] = jax.nn.sigmoid(
            _dot(h_ref[...], wg_ref[:, c * gcols:(c + 1) * gcols]))
    _conv_carry(ext_ref, tm, taps)
    u = cv_ref[...] + cb_ref[...]
    xc = u - jnp.mean(u, axis=-1, keepdims=True)
    ln = xc * lax.rsqrt(jnp.mean(xc * xc, axis=-1, keepdims=True) + EPS) * lng_ref[...] + lnb_ref[...]
    act = (ln * jax.nn.sigmoid(ln)).astype(BF16)
    y = _dot(act, wo_ref[...])
    out_ref[0] = (prev_ref[0].astype(F32) + gate_ref[...] * y).astype(BF16)


def _cf_branch(x, gain, wcf, cw, cb, lng, lnb, prev, wo, wg):
    b, s, d = x.shape
    taps, cf = cw.shape
    tm = min(TOKEN_TILE, s)
    halo = _halo_rows(taps)
    return pl.pallas_call(
        functools.partial(_cf_kernel, taps=taps),
        grid=(b, s // tm),
        in_specs=[_tile_spec(tm, d), _const_spec(gain.shape), _const_spec(wcf.shape),
                  _const_spec(cw.shape), _const_spec(cb.shape), _const_spec(lng.shape),
                  _const_spec(lnb.shape), _tile_spec(tm, d), _const_spec(wo.shape),
                  _const_spec(wg.shape)],
        out_specs=_tile_spec(tm, d),
        out_shape=jax.ShapeDtypeStruct((b, s, d), BF16),
        scratch_shapes=[pltpu.VMEM((halo + tm, cf), F32), pltpu.VMEM((tm, cf), F32),
                        pltpu.VMEM((tm, d), BF16), pltpu.VMEM((tm, d), F32)],
        compiler_params=_params(2),
        name="mixer_conformer_conv",
    )(x, gain, wcf, cw, cb, lng, lnb, prev, wo, wg)


def _memkv_kernel(mem_ref, g_ref, wkv_ref, gk_ref, k_ref, v_ref, *, n_heads, hd):
    xw = n_heads * hd
    mn = _rms(mem_ref[0], g_ref[...]).astype(BF16)
    kv = _dot(mn, wkv_ref[...])
    for hh in range(n_heads):
        k_ref[0, :, hh * hd:(hh + 1) * hd] = _rms(kv[:, hh * hd:(hh + 1) * hd], gk_ref[...]).astype(BF16)
    v_ref[0] = kv[:, xw:].astype(BF16)


def _mem_kv(mem, gain, wkv, gk, *, n_heads, hd):
    b, m, d = mem.shape
    xw = n_heads * hd
    spec_m = pl.BlockSpec((1, m, d), lambda i: (i, 0, 0))
    spec_o = pl.BlockSpec((1, m, xw), lambda i: (i, 0, 0))
    return pl.pallas_call(
        functools.partial(_memkv_kernel, n_heads=n_heads, hd=hd),
        grid=(b,),
        in_specs=[spec_m, _const_spec(gain.shape), _const_spec(wkv.shape), _const_spec(gk.shape)],
        out_specs=[spec_o, spec_o],
        out_shape=[jax.ShapeDtypeStruct((b, m, xw), BF16)] * 2,
        compiler_params=_params(1),
        name="memory_kv",
    )(mem, gain, wkv, gk)


def _xattn_router_kernel(x_ref, mg_ref, wmix_ref, gx_ref, wq_ref, gq_ref, k_ref, v_ref, wo_ref,
                         gf_ref, wr_ref, br_ref, x_out, h_out, route_out, cnt_out, carry_ref,
                         *, n_heads, hd, n_groups, per_group):
    tm = x_ref.shape[1]

    @pl.when(jnp.logical_and(pl.program_id(0) == 0, pl.program_id(1) == 0))
    def _():
        carry_ref[...] = jnp.zeros_like(carry_ref)

    x1 = x_ref[0] + _dot(mg_ref[0], wmix_ref[...])
    q = _dot(_rms(x1, gx_ref[...]).astype(BF16), wq_ref[...])
    outs = []
    for hh in range(n_heads):
        qn = (_rms(q[:, hh * hd:(hh + 1) * hd], gq_ref[...])).astype(BF16)
        sc = _dot_t(qn, k_ref[0, :, hh * hd:(hh + 1) * hd])
        p = jnp.exp(sc - jnp.max(sc, axis=-1, keepdims=True))
        den = jnp.sum(p, axis=-1, keepdims=True)
        outs.append(_dot(p.astype(BF16), v_ref[0, :, hh * hd:(hh + 1) * hd]) / den)
    x2 = x1 + _dot(jnp.concatenate(outs, axis=-1).astype(BF16), wo_ref[...])
    x_out[0] = x2
    hf = _rms(x2, gf_ref[...])
    h_out[0] = _pack_halves(hf)
    lg = _dot(hf.astype(BF16), wr_ref[...]) + br_ref[...]
    lane = lax.broadcasted_iota(I32, (tm, LANES), 1).astype(F32)

    def first_lane(mask):
        return jnp.min(jnp.where(mask, lane, float(LANES)), axis=-1, keepdims=True)

    gmask = lane < n_groups
    gl = jnp.where(gmask, lg, NEG_INF)
    ge = jnp.where(gmask, jnp.exp(gl - jnp.max(gl, axis=-1, keepdims=True)), 0.0)
    pg = ge / jnp.sum(ge, axis=-1, keepdims=True)
    pg_top = jnp.max(pg, axis=-1, keepdims=True)
    g_idx = first_lane((pg == pg_top) & gmask)
    lo = n_groups + g_idx * per_group
    emask = (lane >= lo) & (lane < lo + per_group)
    el = jnp.where(emask, lg, NEG_INF)
    ee = jnp.where(emask, jnp.exp(el - jnp.max(el, axis=-1, keepdims=True)), 0.0)
    pe = ee / jnp.sum(ee, axis=-1, keepdims=True)
    pe = jnp.where(emask, pe, -1.0)
    p1 = jnp.max(pe, axis=-1, keepdims=True)
    i1 = first_lane(pe == p1)
    pe2 = jnp.where(lane == i1, -1.0, pe)
    p2 = jnp.max(pe2, axis=-1, keepdims=True)
    i2 = first_lane((pe2 == p2) & emask & (lane != i1))
    den = p1 + p2
    w1 = pg_top * (p1 / den)
    w2 = pg_top * (p2 / den)
    hit1, hit2 = lane == i1, lane == i2
    assign = jnp.where(hit1 | hit2, 1.0, 0.0)
    ri = lax.broadcasted_iota(I32, (tm, tm), 0)
    rj = lax.broadcasted_iota(I32, (tm, tm), 1)
    lower = jnp.where(rj < ri, 1.0, 0.0).astype(BF16)
    before = _dot(lower, assign.astype(BF16)) + carry_ref[0:1, :]
    rank1 = jnp.sum(jnp.where(hit1, before, 0.0), axis=-1, keepdims=True)
    rank2 = jnp.sum(jnp.where(hit2, before, 0.0), axis=-1, keepdims=True)
    carry_ref[...] = carry_ref[...] + jnp.sum(assign, axis=0, keepdims=True)
    cnt_out[...] = carry_ref[...]
    fields = [i1 - n_groups, i2 - n_groups, rank1, rank2, w1, w2]
    packed = jnp.zeros((tm, LANES), F32)
    for idx, val in enumerate(fields):
        packed = jnp.where(lane == idx, val, packed)
    route_out[0] = packed


def _xattn_router(x, merged, wmix, gx, wq, gq, kx, vx, wo, gf, wr, br, *, n_heads, hd, n_groups,
                  per_group):
    b, s, d = x.shape
    m, xw = kx.shape[1:]
    tm = min(TOKEN_TILE, s)
    kv_spec = pl.BlockSpec((1, m, xw), lambda bb, ss: (bb, 0, 0))
    return pl.pallas_call(
        functools.partial(_xattn_router_kernel, n_heads=n_heads, hd=hd, n_groups=n_groups,
                          per_group=per_group),
        grid=(b, s // tm),
        in_specs=[_tile_spec(tm, d), _tile_spec(tm, d), _const_spec(wmix.shape),
                  _const_spec(gx.shape), _const_spec(wq.shape), _const_spec(gq.shape),
                  kv_spec, kv_spec, _const_spec(wo.shape), _const_spec(gf.shape),
                  _const_spec(wr.shape), _const_spec(br.shape)],
        out_specs=[_tile_spec(tm, d), _tile_spec(tm, d // 2), _tile_spec(tm, LANES),
                   pl.BlockSpec((SUBLANES, LANES), lambda bb, ss: (0, 0))],
        out_shape=[jax.ShapeDtypeStruct((b, s, d), F32), jax.ShapeDtypeStruct((b, s, d // 2), U32),
                   jax.ShapeDtypeStruct((b, s, LANES), F32),
                   jax.ShapeDtypeStruct((SUBLANES, LANES), F32)],
        scratch_shapes=[pltpu.VMEM((SUBLANES, LANES), F32)],
        compiler_params=_params(2),
        name="mixout_xattn_router",
    )(x, merged, wmix, gx, wq, gq, kx, vx, wo, gf, wr, br)


def _dispatch_kernel(seg_end_ref, cnt_ref, used_ref, pos_ref, h_ref, xs_ref, zero_ref, sems, *,
                     n_experts, tile_rows):
    step = pl.program_id(0)
    tokens = pos_ref.shape[0] // TOP_K
    max_tiles = xs_ref.shape[0] // tile_rows

    def fill_copy(row_start):
        start = pl.multiple_of(row_start, tile_rows)
        return pltpu.make_async_copy(zero_ref, xs_ref.at[pl.ds(start, tile_rows)], sems.at[0])

    @pl.when(step == 0)
    def _():
        zero_ref[...] = jnp.zeros_like(zero_ref)
        for e in range(n_experts):
            @pl.when(cnt_ref[e] > 0)
            def _():
                fill_copy(seg_end_ref[e] - tile_rows).start()

        def start_tail(i, c):
            fill_copy(i * tile_rows).start()
            return c

        def wait_tail(i, c):
            fill_copy(i * tile_rows).wait()
            return c

        lax.fori_loop(used_ref[0], max_tiles, start_tail, 0)
        for e in range(n_experts):
            @pl.when(cnt_ref[e] > 0)
            def _():
                fill_copy(seg_end_ref[e] - tile_rows).wait()
        lax.fori_loop(used_ref[0], max_tiles, wait_tail, 0)

    def row_copy(r, k):
        return pltpu.make_async_copy(h_ref.at[pl.ds(r, 1)],
                                     xs_ref.at[pl.ds(pos_ref[TOP_K * r + k], 1)], sems.at[1])

    def start(r, c):
        for k in range(TOP_K):
            row_copy(r, k).start()
        return c

    def wait(r, c):
        for k in range(TOP_K):
            row_copy(r, k).wait()
        return c

    lax.fori_loop(0, tokens, start, 0, unroll=DMA_UNROLL)
    lax.fori_loop(0, tokens, wait, 0, unroll=DMA_UNROLL)


def _dispatch(seg_end, cnt, n_used, pos, h, *, rows_out, tile_rows):
    t, w = h.shape
    tokens = min(TOKEN_TILE, t)
    grid_spec = pltpu.PrefetchScalarGridSpec(
        num_scalar_prefetch=3,
        grid=(t // tokens,),
        in_specs=[pl.BlockSpec((tokens * TOP_K,), lambda i, *_: (i,), memory_space=pltpu.SMEM),
                  pl.BlockSpec((tokens, w), lambda i, *_: (i, 0))],
        out_specs=pl.BlockSpec(memory_space=pl.ANY),
        scratch_shapes=[pltpu.VMEM((tile_rows, w), h.dtype), pltpu.SemaphoreType.DMA((2,))],
    )
    return pl.pallas_call(
        functools.partial(_dispatch_kernel, n_experts=cnt.shape[0], tile_rows=tile_rows),
        grid_spec=grid_spec,
        out_shape=jax.ShapeDtypeStruct((rows_out, w), h.dtype),
        compiler_params=_params(1),
        name="moe_dispatch",
    )(seg_end, cnt, n_used, pos, h)


def _expert_kernel(exp_ref, used_ref, xs_ref, wgu_ref, wdn_ref, ys_ref, wgu_bf, wdn_bf):
    de = wdn_ref.shape[1]
    i = pl.program_id(0)
    live = i < used_ref[0]
    new_expert = jnp.logical_or(i == 0, exp_ref[i] != exp_ref[jnp.maximum(i - 1, 0)])

    @pl.when(jnp.logical_and(live, new_expert))
    def _():
        wgu_bf[...] = wgu_ref[0].astype(BF16)
        wdn_bf[...] = wdn_ref[0].astype(BF16)

    @pl.when(live)
    def _():
        hi, lo = _unpack_halves(xs_ref[...])
        xb = jnp.concatenate([hi, lo], axis=1).astype(BF16)
        gu = _dot(xb, wgu_bf[...])
        act = (jax.nn.silu(gu[:, :de]) * gu[:, de:]).astype(BF16)
        ys_ref[...] = _pack_halves(_dot(act, wdn_bf[...]))

    @pl.when(jnp.logical_not(live))
    def _():
        ys_ref[...] = jnp.zeros_like(ys_ref)


def _experts(tile_exp, n_used, xs, wgu, wdn, *, tile_rows):
    rows, w = xs.shape
    d = 2 * w
    de = wdn.shape[1]
    grid_spec = pltpu.PrefetchScalarGridSpec(
        num_scalar_prefetch=2,
        grid=(rows // tile_rows,),
        in_specs=[pl.BlockSpec((tile_rows, w), lambda i, te, nu: (i, 0)),
                  pl.BlockSpec((1, d, 2 * de), lambda i, te, nu: (te[i], 0, 0)),
                  pl.BlockSpec((1, de, d), lambda i, te, nu: (te[i], 0, 0))],
        out_specs=pl.BlockSpec((tile_rows, w), lambda i, te, nu: (i, 0)),
        scratch_shapes=[pltpu.VMEM((d, 2 * de), BF16), pltpu.VMEM((de, d), BF16)],
    )
    return pl.pallas_call(
        _expert_kernel,
        grid_spec=grid_spec,
        out_shape=jax.ShapeDtypeStruct((rows, w), U32),
        compiler_params=_params(1),
        name="moe_experts",
    )(tile_exp, n_used, xs, wgu, wdn)


def _combine_kernel(pos_ref, x_ref, route_ref, ys_ref, out_ref, buf, sems):
    tokens = x_ref.shape[0]

    def row_copy(r, k):
        return pltpu.make_async_copy(ys_ref.at[pl.ds(pos_ref[TOP_K * r + k], 1)],
                                     buf.at[k, pl.ds(r, 1)], sems.at[k])

    def start(r, c):
        for k in range(TOP_K):
            row_copy(r, k).start()
        return c

    def wait(r, c):
        for k in range(TOP_K):
            row_copy(r, k).wait()
        return c

    lax.fori_loop(0, tokens, start, 0, unroll=DMA_UNROLL)
    lax.fori_loop(0, tokens, wait, 0, unroll=DMA_UNROLL)
    w = x_ref.shape[1] // 2
    acc_hi, acc_lo = x_ref[:, :w], x_ref[:, w:]
    for k in range(TOP_K):
        hi, lo = _unpack_halves(buf[k])
        weight = route_ref[:, 2 * TOP_K + k:2 * TOP_K + k + 1]
        acc_hi = acc_hi + weight * hi
        acc_lo = acc_lo + weight * lo
    out_ref[:, :w] = acc_hi
    out_ref[:, w:] = acc_lo


def _combine(pos, x, route, ys):
    t, d = x.shape
    tokens = min(TOKEN_TILE, t)
    return pl.pallas_call(
        _combine_kernel,
        grid=(t // tokens,),
        in_specs=[pl.BlockSpec((tokens * TOP_K,), lambda i: (i,), memory_space=pltpu.SMEM),
                  pl.BlockSpec((tokens, d), lambda i: (i, 0)),
                  pl.BlockSpec((tokens, LANES), lambda i: (i, 0)),
                  pl.BlockSpec(memory_space=pl.ANY)],
        out_specs=pl.BlockSpec((tokens, d), lambda i: (i, 0)),
        out_shape=jax.ShapeDtypeStruct((t, d), F32),
        scratch_shapes=[pltpu.VMEM((TOP_K, tokens, d // 2), U32),
                        pltpu.SemaphoreType.DMA((TOP_K,))],
        compiler_params=_params(1),
        name="moe_combine",
    )(pos, x, route, ys)


def _dup_heads(w, n_heads, hd):
    d = w.shape[0]
    w3 = w.reshape(d, n_heads, hd)
    return jnp.concatenate([w3, w3], axis=-1).reshape(d, n_heads * 2 * hd)


def _row(v):
    return v.reshape(1, -1).astype(F32)


def kernel(x, mem, mix_norm, w_in, sc_conv, attn_q_gain, attn_k_gain, attn_sinks, w_out_sc, w_out_attn, cf_conv, cf_conv_b, cf_norm_g, cf_norm_b, w_out_cf, w_mix_out, xa_norm, mem_norm, xa_wq, xa_wkv, xa_q_gain, xa_k_gain, xa_wo, ffn_norm, w_route_group, b_route_group, w_route_expert, b_route_expert, w_expert_gu, w_expert_down):
    b, s, d = x.shape
    depth = w_in.shape[0]
    sc = sc_conv.shape[-1]
    cf = cf_conv.shape[-1]
    hd = attn_q_gain.shape[-1]
    n_q = attn_sinks.shape[-1]
    qw = n_q * hd
    kvw = (w_in.shape[-1] - 3 * sc - qw - 2 * cf - 3 * d) // 2
    n_kv = kvw // hd
    x_hd = xa_q_gain.shape[-1]
    x_heads = xa_wq.shape[-1] // x_hd
    n_groups = w_route_group.shape[-1]
    n_experts = w_route_expert.shape[-1]
    per_group = n_experts // n_groups
    assert 2 * hd == LANES and n_q % (2 * n_kv) == 0 and x_hd == LANES
    assert n_groups + n_experts <= LANES and s % ATTN_BLOCK == 0
    t = b * s
    tile_rows = EXPERT_TILE
    max_tiles = (t * TOP_K) // tile_rows + n_experts

    o_q = 3 * sc
    o_k = o_q + qw
    o_v = o_k + kvw
    o_cf = o_v + kvw
    o_g = o_cf + 2 * cf
    gq = jnp.concatenate([attn_q_gain, attn_q_gain], axis=-1) * (hd ** -0.5)
    gk = jnp.concatenate([attn_k_gain, attn_k_gain], axis=-1)

    for l in range(depth):
        wl = w_in[l].astype(BF16)
        wqkv = jnp.concatenate([wl[:, o_q:o_k], _dup_heads(wl[:, o_k:o_v], n_kv, hd),
                                _dup_heads(wl[:, o_v:o_cf], n_kv, hd)], axis=-1)
        gmix = _row(mix_norm[l])
        m1 = _sc_branch(x, gmix, wl[:, :o_q], sc_conv[l], w_out_sc[l].astype(BF16),
                        wl[:, o_g:o_g + d])
        m2 = _swa_branch(x, gmix, wqkv, _row(gq[l]), _row(gk[l]), attn_sinks[l], m1,
                         w_out_attn[l].astype(BF16), wl[:, o_g + d:o_g + 2 * d],
                         n_q=n_q, n_kv=n_kv, hd=hd)
        m3 = _cf_branch(x, gmix, wl[:, o_cf:o_g], cf_conv[l], _row(cf_conv_b[l]),
                        _row(cf_norm_g[l]), _row(cf_norm_b[l]), m2, w_out_cf[l].astype(BF16),
                        wl[:, o_g + 2 * d:o_g + 3 * d])
        kx, vx = _mem_kv(mem, _row(mem_norm[l]), xa_wkv[l].astype(BF16), _row(xa_k_gain[l]),
                         n_heads=x_heads, hd=x_hd)
        wr = jnp.concatenate(
            [w_route_group[l], w_route_expert[l],
             jnp.zeros((d, LANES - n_groups - n_experts), F32)], axis=-1).astype(BF16)
        br = jnp.concatenate(
            [b_route_group[l], b_route_expert[l],
             jnp.zeros((LANES - n_groups - n_experts,), F32)]).reshape(1, LANES)
        x2, hf, route, counts = _xattn_router(
            x, m3, w_mix_out[l].astype(BF16), _row(xa_norm[l]), xa_wq[l].astype(BF16),
            _row(xa_q_gain[l] * (x_hd ** -0.5)), kx, vx, xa_wo[l].astype(BF16),
            _row(ffn_norm[l]), wr, br, n_heads=x_heads, hd=x_hd, n_groups=n_groups,
            per_group=per_group)
        cnt = counts[0, n_groups:n_groups + n_experts].astype(I32)
        n_tile = (cnt + tile_rows - 1) // tile_rows
        tile_end = jnp.cumsum(n_tile)
        seg_start = (tile_end - n_tile) * tile_rows
        route2 = route.reshape(t, LANES)
        eid = route2[:, 0:TOP_K].astype(I32)
        rank = route2[:, TOP_K:2 * TOP_K].astype(I32)
        experts = jnp.arange(n_experts, dtype=I32)
        pos = (jnp.sum(jnp.where(eid[..., None] == experts, seg_start, 0), axis=-1)
               + rank).reshape(t * TOP_K)
        n_used = tile_end[-1:].astype(I32)
        tile_live = jnp.minimum(jnp.arange(max_tiles, dtype=I32), n_used - 1)
        tile_exp = jnp.minimum(jnp.sum(tile_end[None, :] <= tile_live[:, None], axis=1),
                               n_experts - 1).astype(I32)
        xs = _dispatch(tile_end * tile_rows, cnt, n_used, pos, hf.reshape(t, d // 2),
                       rows_out=max_tiles * tile_rows, tile_rows=tile_rows)
        ys = _experts(tile_exp, n_used, xs, w_expert_gu[l], w_expert_down[l],
                      tile_rows=tile_rows)
        x = _combine(pos, x2.reshape(t, d), route2, ys).reshape(b, s, d)
    return x
```

```python
import functools

import jax
import jax.numpy as jnp
from jax import lax
from jax.experimental import pallas as pl
from jax.experimental.pallas import tpu as pltpu

F32 = jnp.float32
BF16 = jnp.bfloat16
I32 = jnp.int32
U32 = jnp.uint32

EPS = 1e-6
NEG_INF = -1e30
ATTN_BLOCK = 128
TOP_K = 2
LANES = 128
SUBLANES = 8
TOKEN_TILE = 256
EXPERT_TILE = 256
DMA_UNROLL = 8
VMEM_LIMIT = 56 * 1024 * 1024


def _const_spec(shape):
    nd = len(shape)
    return pl.BlockSpec(shape, lambda *_: (0,) * nd, pipeline_mode=pl.Buffered(1))


def _tile_spec(tm, width):
    return pl.BlockSpec((1, tm, width), lambda b, s: (b, s, 0))


def _params(n_axes):
    return pltpu.CompilerParams(dimension_semantics=("arbitrary",) * n_axes,
                                vmem_limit_bytes=VMEM_LIMIT)


def _rms(xf, gain):
    return xf * lax.rsqrt(jnp.mean(xf * xf, axis=-1, keepdims=True) + EPS) * gain


def _dot(a, b):
    return jnp.dot(a, b, preferred_element_type=F32)


def _dot_t(a, b):
    return lax.dot_general(a, b, (((1,), (1,)), ((), ())), preferred_element_type=F32)


def _pack_halves(v):
    w = v.shape[1] // 2
    hi = pltpu.bitcast(v[:, :w].astype(BF16).astype(F32), U32)
    lo = pltpu.bitcast(v[:, w:].astype(BF16).astype(F32), U32)
    return hi | (lo >> 16)


def _unpack_halves(u):
    return pltpu.bitcast(u & jnp.uint32(0xFFFF0000), F32), pltpu.bitcast(u << 16, F32)


def _zero_after(v, shape):
    bits = pltpu.bitcast(v[0:SUBLANES, 0:LANES], U32)
    zero = pltpu.bitcast((bits >> 16) >> 16, F32)
    return jnp.tile(zero, (shape[0] // SUBLANES, shape[1] // LANES))


def _halo_rows(taps):
    return -(-(taps - 1) // SUBLANES) * SUBLANES


def _conv_cols(ext_ref, w_ref, out_ref, tm, taps, cols, row_base=0):
    halo = _halo_rows(taps)
    rows = min(64, tm)
    groups = {}
    for k in range(taps):
        off = halo - (taps - 1) + k
        groups.setdefault(off % SUBLANES, []).append((off // SUBLANES, k))
    for r0 in range(row_base, row_base + tm, rows):
        total = None
        for mis in sorted(groups):
            members = groups[mis]
            span = SUBLANES * max(a for a, _ in members) + rows
            shifted = ext_ref[pl.ds(r0 + mis, span), cols]
            part = None
            for a, k in members:
                term = w_ref[pl.ds(k, 1), cols] * shifted[SUBLANES * a:SUBLANES * a + rows]
                part = term if part is None else part + term
            total = part if total is None else total + part
        out_ref[pl.ds(r0, rows), cols] = total


def _conv_carry(ext_ref, tm, taps):
    halo = _halo_rows(taps)
    ext_ref[0:halo, :] = ext_ref[tm:tm + halo, :]


def _sc_kernel(x_ref, g_ref, wsc_ref, cw_ref, wo_ref, wg_ref, out_ref, ext_ref, cv_ref, *, taps):
    tm = x_ref.shape[1]
    sc = cw_ref.shape[1]
    halo = _halo_rows(taps)

    @pl.when(pl.program_id(1) == 0)
    def _():
        ext_ref[0:halo, :] = jnp.zeros((halo, sc), F32)

    h = _rms(x_ref[0], g_ref[...]).astype(BF16)
    p = _dot(h, wsc_ref[...])
    ext_ref[halo:halo + tm, :] = p[:, sc:2 * sc] * p[:, 2 * sc:]

    def col_chunk(c, carry):
        _conv_cols(ext_ref, cw_ref, cv_ref, tm, taps, pl.ds(pl.multiple_of(c * LANES, LANES), LANES))
        return carry

    lax.fori_loop(0, sc // LANES, col_chunk, 0)
    _conv_carry(ext_ref, tm, taps)
    v = (p[:, :sc] * cv_ref[...]).astype(BF16)
    y = _dot(v, wo_ref[...])
    gate = jax.nn.sigmoid(_dot(h, wg_ref[...]))
    out_ref[0] = (gate * y).astype(BF16)


def _sc_branch(x, gain, wsc, cw, wo, wg):
    b, s, d = x.shape
    taps, sc = cw.shape
    tm = min(TOKEN_TILE, s)
    halo = _halo_rows(taps)
    return pl.pallas_call(
        functools.partial(_sc_kernel, taps=taps),
        grid=(b, s // tm),
        in_specs=[_tile_spec(tm, d), _const_spec(gain.shape), _const_spec(wsc.shape),
                  _const_spec(cw.shape), _const_spec(wo.shape), _const_spec(wg.shape)],
        out_specs=_tile_spec(tm, d),
        out_shape=jax.ShapeDtypeStruct((b, s, d), BF16),
        scratch_shapes=[pltpu.VMEM((halo + tm, sc), F32), pltpu.VMEM((tm, sc), F32)],
        compiler_params=_params(2),
        name="mixer_gated_conv",
    )(x, gain, wsc, cw, wo, wg)


def _swa_kernel(sink_ref, x_ref, g_ref, wqkv_ref, gq_ref, gk_ref, prev_ref, wo_ref, wg_ref,
                out_ref, kbuf, vbuf, attn_ref, *, n_q, n_kv, hd):
    tm = x_ref.shape[1]
    blk = ATTN_BLOCK
    grp = n_q // n_kv
    qw = n_q * hd
    kw = n_kv * LANES
    s_idx = pl.program_id(1)

    @pl.when(s_idx == 0)
    def _():
        kbuf[0:blk, :] = jnp.zeros((blk, kw), BF16)
        vbuf[0:blk, :] = jnp.zeros((blk, kw), BF16)

    h = _rms(x_ref[0], g_ref[...]).astype(BF16)
    qkv = _dot(h, wqkv_ref[...])
    for g in range(n_kv):
        ks = qkv[:, qw + g * LANES:qw + (g + 1) * LANES]
        ms = jnp.sum(ks * ks, axis=-1, keepdims=True) * (1.0 / LANES)
        kbuf[blk:blk + tm, g * LANES:(g + 1) * LANES] = (
            ks * lax.rsqrt(ms + EPS) * gk_ref[...]).astype(BF16)
    vbuf[blk:blk + tm, :] = qkv[:, qw + kw:].astype(BF16)

    lane = lax.broadcasted_iota(I32, (blk, LANES), 1)
    qi = lax.broadcasted_iota(I32, (blk, 2 * blk), 0)
    kj = lax.broadcasted_iota(I32, (blk, 2 * blk), 1)
    for jb in range(tm // blk):
        r0 = jb * blk
        first = jnp.logical_and(s_idx == 0, jb == 0)
        lo = jnp.where(first, blk, 0)
        valid = (kj >= jnp.maximum(qi + 1, lo)) & (kj <= qi + blk)
        for g in range(n_kv):
            kk = kbuf[r0:r0 + 2 * blk, g * LANES:(g + 1) * LANES]
            vv = vbuf[r0:r0 + 2 * blk, g * LANES:(g + 1) * LANES]
            qn = []
            for r in range(grp):
                head = g * grp + r
                slab, half = head // 2, head % 2
                qs = qkv[r0:r0 + blk, slab * LANES:(slab + 1) * LANES]
                qm = jnp.where(lane >= hd if half else lane < hd, qs, 0.0)
                ms = jnp.sum(qm * qm, axis=-1, keepdims=True) * (1.0 / hd)
                qn.append((qm * lax.rsqrt(ms + EPS) * gq_ref[...]).astype(BF16))
            scores = _dot_t(jnp.concatenate(qn, axis=0), kk)
            probs, dens = [], []
            for r in range(grp):
                sink = sink_ref[g * grp + r]
                sc = jnp.where(valid, scores[r * blk:(r + 1) * blk], NEG_INF)
                m = jnp.maximum(jnp.max(sc, axis=-1, keepdims=True), sink)
                p = jnp.exp(sc - m)
                dens.append(jnp.sum(p, axis=-1, keepdims=True) + jnp.exp(sink - m))
                probs.append(p.astype(BF16))
            o = _dot(jnp.concatenate(probs, axis=0), vv)
            outs = [o[r * blk:(r + 1) * blk] / dens[r] for r in range(grp)]
            for pr in range(grp // 2):
                slab = (g * grp) // 2 + pr
                attn_ref[r0:r0 + blk, slab * LANES:(slab + 1) * LANES] = jnp.where(
                    lane < hd, outs[2 * pr], outs[2 * pr + 1])

    kbuf[0:blk, :] = kbuf[tm:tm + blk, :]
    vbuf[0:blk, :] = vbuf[tm:tm + blk, :]
    y = _dot(attn_ref[...].astype(BF16), wo_ref[...])
    gate = jax.nn.sigmoid(_dot(h, wg_ref[...]))
    out_ref[0] = (prev_ref[0].astype(F32) + gate * y).astype(BF16)


def _swa_branch(x, gain, wqkv, gq, gk, sinks, prev, wo, wg, *, n_q, n_kv, hd):
    b, s, d = x.shape
    tm = min(TOKEN_TILE, s)
    kw = n_kv * LANES
    tile = lambda bb, ss, *_: (bb, ss, 0)
    const = lambda shape: pl.BlockSpec(shape, lambda *_: (0,) * len(shape),
                                       pipeline_mode=pl.Buffered(1))
    grid_spec = pltpu.PrefetchScalarGridSpec(
        num_scalar_prefetch=1,
        grid=(b, s // tm),
        in_specs=[pl.BlockSpec((1, tm, d), tile), const(gain.shape), const(wqkv.shape),
                  const(gq.shape), const(gk.shape), pl.BlockSpec((1, tm, d), tile),
                  const(wo.shape), const(wg.shape)],
        out_specs=pl.BlockSpec((1, tm, d), tile),
        scratch_shapes=[pltpu.VMEM((ATTN_BLOCK + tm, kw), BF16),
                        pltpu.VMEM((ATTN_BLOCK + tm, kw), BF16),
                        pltpu.VMEM((tm, n_q * hd), F32)],
    )
    return pl.pallas_call(
        functools.partial(_swa_kernel, n_q=n_q, n_kv=n_kv, hd=hd),
        grid_spec=grid_spec,
        out_shape=jax.ShapeDtypeStruct((b, s, d), BF16),
        compiler_params=_params(2),
        name="mixer_swa",
    )(sinks, x, gain, wqkv, gq, gk, prev, wo, wg)


def _cf_kernel(x_ref, g_ref, wcf_ref, cw_ref, cb_ref, lng_ref, lnb_ref, prev_ref, wo_ref, wg_ref,
               out_ref, ext_ref, cv_ref, h_ref, gate_ref, *, taps):
    tm, d = x_ref.shape[1:]
    cf = cw_ref.shape[1]
    halo = _halo_rows(taps)

    @pl.when(pl.program_id(1) == 0)
    def _():
        ext_ref[0:halo, :] = jnp.zeros((halo, cf), F32)

    h_ref[...] = _rms(x_ref[0], g_ref[...]).astype(BF16)
    width = 2 * LANES
    gwidth = d * width // cf
    for ci, c0 in enumerate(range(0, cf, width)):
        val = _dot(h_ref[...], wcf_ref[:, c0:c0 + width])
        gl = _dot(h_ref[...], wcf_ref[:, cf + c0:cf + c0 + width])
        gate = jax.nn.sigmoid(_dot(h_ref[...], wg_ref[:, ci * gwidth:(ci + 1) * gwidth]))
        gate_ref[:, ci * gwidth:(ci + 1) * gwidth] = gate
        ext_ref[halo:halo + tm, c0:c0 + width] = (
            val * jax.nn.sigmoid(gl) + _zero_after(gate, (tm, width)))
        for c in range(c0, c0 + width, LANES):
            _conv_cols(ext_ref, cw_ref, cv_ref, tm, taps, slice(c, c + LANES))
    _conv_carry(ext_ref, tm, taps)
    u = cv_ref[...] + cb_ref[...]
    xc = u - jnp.mean(u, axis=-1, keepdims=True)
    ln = xc * lax.rsqrt(jnp.mean(xc * xc, axis=-1, keepdims=True) + EPS) * lng_ref[...] + lnb_ref[...]
    act = (ln * jax.nn.sigmoid(ln)).astype(BF16)
    y = _dot(act, wo_ref[...])
    out_ref[0] = (prev_ref[0].astype(F32) + gate_ref[...] * y).astype(BF16)


def _cf_branch(x, gain, wcf, cw, cb, lng, lnb, prev, wo, wg):
    b, s, d = x.shape
    taps, cf = cw.shape
    tm = min(TOKEN_TILE, s)
    halo = _halo_rows(taps)
    return pl.pallas_call(
        functools.partial(_cf_kernel, taps=taps),
        grid=(b, s // tm),
        in_specs=[_tile_spec(tm, d), _const_spec(gain.shape), _const_spec(wcf.shape),
                  _const_spec(cw.shape), _const_spec(cb.shape), _const_spec(lng.shape),
                  _const_spec(lnb.shape), _tile_spec(tm, d), _const_spec(wo.shape),
                  _const_spec(wg.shape)],
        out_specs=_tile_spec(tm, d),
        out_shape=jax.ShapeDtypeStruct((b, s, d), BF16),
        scratch_shapes=[pltpu.VMEM((halo + tm, cf), F32), pltpu.VMEM((tm, cf), F32),
                        pltpu.VMEM((tm, d), BF16), pltpu.VMEM((tm, d), F32)],
        compiler_params=_params(2),
        name="mixer_conformer_conv",
    )(x, gain, wcf, cw, cb, lng, lnb, prev, wo, wg)


def _memkv_kernel(mem_ref, g_ref, wkv_ref, gk_ref, k_ref, v_ref, *, n_heads, hd):
    xw = n_heads * hd
    mn = _rms(mem_ref[0], g_ref[...]).astype(BF16)
    kv = _dot(mn, wkv_ref[...])
    for hh in range(n_heads):
        k_ref[0, :, hh * hd:(hh + 1) * hd] = _rms(kv[:, hh * hd:(hh + 1) * hd], gk_ref[...]).astype(BF16)
    v_ref[0] = kv[:, xw:].astype(BF16)


def _mem_kv(mem, gain, wkv, gk, *, n_heads, hd):
    b, m, d = mem.shape
    xw = n_heads * hd
    spec_m = pl.BlockSpec((1, m, d), lambda i: (i, 0, 0))
    spec_o = pl.BlockSpec((1, m, xw), lambda i: (i, 0, 0))
    return pl.pallas_call(
        functools.partial(_memkv_kernel, n_heads=n_heads, hd=hd),
        grid=(b,),
        in_specs=[spec_m, _const_spec(gain.shape), _const_spec(wkv.shape), _const_spec(gk.shape)],
        out_specs=[spec_o, spec_o],
        out_shape=[jax.ShapeDtypeStruct((b, m, xw), BF16)] * 2,
        compiler_params=_params(1),
        name="memory_kv",
    )(mem, gain, wkv, gk)


def _xattn_router_kernel(x_ref, mg_ref, wmix_ref, gx_ref, wq_ref, gq_ref, k_ref, v_ref, wo_ref,
                         gf_ref, wr_ref, br_ref, x_out, h_out, route_out, cnt_out, carry_ref,
                         *, n_heads, hd, n_groups, per_group, n_sub):
    @pl.when(jnp.logical_and(pl.program_id(0) == 0, pl.program_id(1) == 0))
    def _():
        carry_ref[...] = jnp.zeros_like(carry_ref)

    sub = x_ref.shape[1] // n_sub
    for i in range(n_sub):
        _xattn_router_rows(slice(i * sub, (i + 1) * sub), x_ref, mg_ref, wmix_ref, gx_ref, wq_ref,
                           gq_ref, k_ref, v_ref, wo_ref, gf_ref, wr_ref, br_ref, x_out, h_out,
                           route_out, carry_ref, n_heads=n_heads, hd=hd, n_groups=n_groups,
                           per_group=per_group)
    cnt_out[...] = carry_ref[...]


def _xattn_router_rows(rows, x_ref, mg_ref, wmix_ref, gx_ref, wq_ref, gq_ref, k_ref, v_ref, wo_ref,
                       gf_ref, wr_ref, br_ref, x_out, h_out, route_out, carry_ref,
                       *, n_heads, hd, n_groups, per_group):
    tm = rows.stop - rows.start
    x1 = x_ref[0, rows, :] + _dot(mg_ref[0, rows, :], wmix_ref[...])
    q = _dot(_rms(x1, gx_ref[...]).astype(BF16), wq_ref[...])
    outs = []
    for hh in range(n_heads):
        qn = (_rms(q[:, hh * hd:(hh + 1) * hd], gq_ref[...])).astype(BF16)
        sc = _dot_t(qn, k_ref[0, :, hh * hd:(hh + 1) * hd])
        p = jnp.exp(sc - jnp.max(sc, axis=-1, keepdims=True))
        den = jnp.sum(p, axis=-1, keepdims=True)
        outs.append(_dot(p.astype(BF16), v_ref[0, :, hh * hd:(hh + 1) * hd]) / den)
    x2 = x1 + _dot(jnp.concatenate(outs, axis=-1).astype(BF16), wo_ref[...])
    x_out[0, rows, :] = x2
    hf = _rms(x2, gf_ref[...])
    h_out[0, rows, :] = _pack_halves(hf)
    lg = _dot(hf.astype(BF16), wr_ref[...]) + br_ref[...]
    lane = lax.broadcasted_iota(I32, (tm, LANES), 1).astype(F32)

    def first_lane(mask):
        return jnp.min(jnp.where(mask, lane, float(LANES)), axis=-1, keepdims=True)

    gmask = lane < n_groups
    gl = jnp.where(gmask, lg, NEG_INF)
    ge = jnp.where(gmask, jnp.exp(gl - jnp.max(gl, axis=-1, keepdims=True)), 0.0)
    pg = ge / jnp.sum(ge, axis=-1, keepdims=True)
    pg_top = jnp.max(pg, axis=-1, keepdims=True)
    g_idx = first_lane((pg == pg_top) & gmask)
    lo = n_groups + g_idx * per_group
    emask = (lane >= lo) & (lane < lo + per_group)
    el = jnp.where(emask, lg, NEG_INF)
    ee = jnp.where(emask, jnp.exp(el - jnp.max(el, axis=-1, keepdims=True)), 0.0)
    pe = ee / jnp.sum(ee, axis=-1, keepdims=True)
    pe = jnp.where(emask, pe, -1.0)
    p1 = jnp.max(pe, axis=-1, keepdims=True)
    i1 = first_lane(pe == p1)
    pe2 = jnp.where(lane == i1, -1.0, pe)
    p2 = jnp.max(pe2, axis=-1, keepdims=True)
    i2 = first_lane((pe2 == p2) & emask & (lane != i1))
    den = p1 + p2
    w1 = pg_top * (p1 / den)
    w2 = pg_top * (p2 / den)
    hit1, hit2 = lane == i1, lane == i2
    assign = jnp.where(hit1 | hit2, 1.0, 0.0)
    ri = lax.broadcasted_iota(I32, (tm, tm), 0)
    rj = lax.broadcasted_iota(I32, (tm, tm), 1)
    lower = jnp.where(rj < ri, 1.0, 0.0).astype(BF16)
    before = _dot(lower, assign.astype(BF16)) + carry_ref[0:1, :]
    rank1 = jnp.sum(jnp.where(hit1, before, 0.0), axis=-1, keepdims=True)
    rank2 = jnp.sum(jnp.where(hit2, before, 0.0), axis=-1, keepdims=True)
    carry_ref[...] = carry_ref[...] + jnp.sum(assign, axis=0, keepdims=True)
    fields = [i1 - n_groups, i2 - n_groups, rank1, rank2, w1, w2]
    packed = jnp.zeros((tm, LANES), F32)
    for idx, val in enumerate(fields):
        packed = jnp.where(lane == idx, val, packed)
    route_out[0, rows, :] = packed


def _xattn_router(x, merged, wmix, gx, wq, gq, kx, vx, wo, gf, wr, br, *, n_heads, hd, n_groups,
                  per_group):
    b, s, d = x.shape
    m, xw = kx.shape[1:]
    tm, n_sub = min(TOKEN_TILE, s), 1
    kv_spec = pl.BlockSpec((1, m, xw), lambda bb, ss: (bb, 0, 0))
    return pl.pallas_call(
        functools.partial(_xattn_router_kernel, n_heads=n_heads, hd=hd, n_groups=n_groups,
                          per_group=per_group, n_sub=n_sub),
        grid=(b, s // tm),
        in_specs=[_tile_spec(tm, d), _tile_spec(tm, d), _const_spec(wmix.shape),
                  _const_spec(gx.shape), _const_spec(wq.shape), _const_spec(gq.shape),
                  kv_spec, kv_spec, _const_spec(wo.shape), _const_spec(gf.shape),
                  _const_spec(wr.shape), _const_spec(br.shape)],
        out_specs=[_tile_spec(tm, d), _tile_spec(tm, d // 2), _tile_spec(tm, LANES),
                   pl.BlockSpec((SUBLANES, LANES), lambda bb, ss: (0, 0))],
        out_shape=[jax.ShapeDtypeStruct((b, s, d), F32), jax.ShapeDtypeStruct((b, s, d // 2), U32),
                   jax.ShapeDtypeStruct((b, s, LANES), F32),
                   jax.ShapeDtypeStruct((SUBLANES, LANES), F32)],
        scratch_shapes=[pltpu.VMEM((SUBLANES, LANES), F32)],
        compiler_params=_params(2),
        name="mixout_xattn_router",
    )(x, merged, wmix, gx, wq, gq, kx, vx, wo, gf, wr, br)


def _dispatch_kernel(seg_end_ref, cnt_ref, used_ref, pos_ref, h_ref, xs_ref, zero_ref, sems, *,
                     n_experts, tile_rows):
    step = pl.program_id(0)
    tokens = pos_ref.shape[0] // TOP_K
    max_tiles = xs_ref.shape[0] // tile_rows

    def fill_copy(row_start):
        start = pl.multiple_of(row_start, tile_rows)
        return pltpu.make_async_copy(zero_ref, xs_ref.at[pl.ds(start, tile_rows)], sems.at[0])

    @pl.when(step == 0)
    def _():
        zero_ref[...] = jnp.zeros_like(zero_ref)
        for e in range(n_experts):
            @pl.when(cnt_ref[e] > 0)
            def _():
                fill_copy(seg_end_ref[e] - tile_rows).start()

        def start_tail(i, c):
            fill_copy(i * tile_rows).start()
            return c

        def wait_tail(i, c):
            fill_copy(i * tile_rows).wait()
            return c

        lax.fori_loop(used_ref[0], max_tiles, start_tail, 0)
        for e in range(n_experts):
            @pl.when(cnt_ref[e] > 0)
            def _():
                fill_copy(seg_end_ref[e] - tile_rows).wait()
        lax.fori_loop(used_ref[0], max_tiles, wait_tail, 0)

    def row_copy(r, k):
        return pltpu.make_async_copy(h_ref.at[pl.ds(r, 1)],
                                     xs_ref.at[pl.ds(pos_ref[TOP_K * r + k], 1)], sems.at[1])

    def start(r, c):
        for k in range(TOP_K):
            row_copy(r, k).start()
        return c

    def wait(r, c):
        for k in range(TOP_K):
            row_copy(r, k).wait()
        return c

    lax.fori_loop(0, tokens, start, 0, unroll=DMA_UNROLL)
    lax.fori_loop(0, tokens, wait, 0, unroll=DMA_UNROLL)


def _dispatch(seg_end, cnt, n_used, pos, h, *, rows_out, tile_rows):
    t, w = h.shape
    tokens = min(TOKEN_TILE, t)
    grid_spec = pltpu.PrefetchScalarGridSpec(
        num_scalar_prefetch=3,
        grid=(t // tokens,),
        in_specs=[pl.BlockSpec((tokens * TOP_K,), lambda i, *_: (i,), memory_space=pltpu.SMEM),
                  pl.BlockSpec((tokens, w), lambda i, *_: (i, 0))],
        out_specs=pl.BlockSpec(memory_space=pl.ANY),
        scratch_shapes=[pltpu.VMEM((tile_rows, w), h.dtype), pltpu.SemaphoreType.DMA((2,))],
    )
    return pl.pallas_call(
        functools.partial(_dispatch_kernel, n_experts=cnt.shape[0], tile_rows=tile_rows),
        grid_spec=grid_spec,
        out_shape=jax.ShapeDtypeStruct((rows_out, w), h.dtype),
        compiler_params=_params(1),
        name="moe_dispatch",
    )(seg_end, cnt, n_used, pos, h)


def _expert_kernel(exp_ref, next_ref, used_ref, xs_ref, wgu_hbm, wdn_hbm, ys_ref, gu_stage, dn_stage,
                   wgu_bf, wdn_bf, sems, *, layer):
    de = wdn_bf.shape[0]
    i = pl.program_id(0)
    live = i < used_ref[0]
    expert = exp_ref[i]
    new_expert = jnp.logical_or(i == 0, expert != exp_ref[jnp.maximum(i - 1, 0)])

    def fetch(e):
        return (pltpu.make_async_copy(wgu_hbm.at[layer, e], gu_stage, sems.at[0]),
                pltpu.make_async_copy(wdn_hbm.at[layer, e], dn_stage, sems.at[1]))

    @pl.when(jnp.logical_and(live, new_expert))
    def _():
        @pl.when(i == 0)
        def _():
            for cp in fetch(expert):
                cp.start()
        for cp in fetch(expert):
            cp.wait()
        wgu_bf[...] = gu_stage[...].astype(BF16)
        wdn_bf[...] = dn_stage[...].astype(BF16)

        @pl.when(next_ref[i] >= 0)
        def _():
            for cp in fetch(next_ref[i]):
                cp.start()

    @pl.when(live)
    def _():
        hi, lo = _unpack_halves(xs_ref[...])
        xb = jnp.concatenate([hi, lo], axis=1).astype(BF16)
        gu = _dot(xb, wgu_bf[...])
        act = (jax.nn.silu(gu[:, :de]) * gu[:, de:]).astype(BF16)
        ys_ref[...] = _pack_halves(_dot(act, wdn_bf[...]))

    @pl.when(jnp.logical_not(live))
    def _():
        ys_ref[...] = jnp.zeros_like(ys_ref)


def _experts(tile_exp, tile_next, n_used, xs, wgu, wdn, *, layer, tile_rows):
    rows, w = xs.shape
    d = 2 * w
    de = wdn.shape[2]
    grid_spec = pltpu.PrefetchScalarGridSpec(
        num_scalar_prefetch=3,
        grid=(rows // tile_rows,),
        in_specs=[pl.BlockSpec((tile_rows, w), lambda i, *_: (i, 0)),
                  pl.BlockSpec(memory_space=pl.ANY), pl.BlockSpec(memory_space=pl.ANY)],
        out_specs=pl.BlockSpec((tile_rows, w), lambda i, *_: (i, 0)),
        scratch_shapes=[pltpu.VMEM((d, 2 * de), F32), pltpu.VMEM((de, d), F32),
                        pltpu.VMEM((d, 2 * de), BF16), pltpu.VMEM((de, d), BF16),
                        pltpu.SemaphoreType.DMA((2,))],
    )
    return pl.pallas_call(
        functools.partial(_expert_kernel, layer=layer),
        grid_spec=grid_spec,
        out_shape=jax.ShapeDtypeStruct((rows, w), U32),
        compiler_params=_params(1),
        name="moe_experts",
    )(tile_exp, tile_next, n_used, xs, wgu, wdn)


def _combine_kernel(pos_ref, x_ref, route_ref, ys_ref, out_ref, buf, sems):
    tokens = x_ref.shape[0]

    def row_copy(r, k):
        return pltpu.make_async_copy(ys_ref.at[pl.ds(pos_ref[TOP_K * r + k], 1)],
                                     buf.at[k, pl.ds(r, 1)], sems.at[k])

    def start(r, c):
        for k in range(TOP_K):
            row_copy(r, k).start()
        return c

    def wait(r, c):
        for k in range(TOP_K):
            row_copy(r, k).wait()
        return c

    lax.fori_loop(0, tokens, start, 0, unroll=DMA_UNROLL)
    lax.fori_loop(0, tokens, wait, 0, unroll=DMA_UNROLL)
    w = x_ref.shape[1] // 2
    acc_hi, acc_lo = x_ref[:, :w], x_ref[:, w:]
    for k in range(TOP_K):
        hi, lo = _unpack_halves(buf[k])
        weight = route_ref[:, 2 * TOP_K + k:2 * TOP_K + k + 1]
        acc_hi = acc_hi + weight * hi
        acc_lo = acc_lo + weight * lo
    out_ref[:, :w] = acc_hi
    out_ref[:, w:] = acc_lo


def _combine(pos, x, route, ys):
    t, d = x.shape
    tokens = min(TOKEN_TILE, t)
    return pl.pallas_call(
        _combine_kernel,
        grid=(t // tokens,),
        in_specs=[pl.BlockSpec((tokens * TOP_K,), lambda i: (i,), memory_space=pltpu.SMEM),
                  pl.BlockSpec((tokens, d), lambda i: (i, 0)),
                  pl.BlockSpec((tokens, LANES), lambda i: (i, 0)),
                  pl.BlockSpec(memory_space=pl.ANY)],
        out_specs=pl.BlockSpec((tokens, d), lambda i: (i, 0)),
        out_shape=jax.ShapeDtypeStruct((t, d), F32),
        scratch_shapes=[pltpu.VMEM((TOP_K, tokens, d // 2), U32),
                        pltpu.SemaphoreType.DMA((TOP_K,))],
        compiler_params=_params(1),
        name="moe_combine",
    )(pos, x, route, ys)


def _dup_heads(w, n_heads, hd):
    d = w.shape[0]
    w3 = w.reshape(d, n_heads, hd)
    return jnp.concatenate([w3, w3], axis=-1).reshape(d, n_heads * 2 * hd)


def _row(v):
    return v.reshape(1, -1).astype(F32)


def kernel(x, mem, mix_norm, w_in, sc_conv, attn_q_gain, attn_k_gain, attn_sinks, w_out_sc, w_out_attn, cf_conv, cf_conv_b, cf_norm_g, cf_norm_b, w_out_cf, w_mix_out, xa_norm, mem_norm, xa_wq, xa_wkv, xa_q_gain, xa_k_gain, xa_wo, ffn_norm, w_route_group, b_route_group, w_route_expert, b_route_expert, w_expert_gu, w_expert_down):
    b, s, d = x.shape
    depth = w_in.shape[0]
    sc = sc_conv.shape[-1]
    cf = cf_conv.shape[-1]
    hd = attn_q_gain.shape[-1]
    n_q = attn_sinks.shape[-1]
    qw = n_q * hd
    kvw = (w_in.shape[-1] - 3 * sc - qw - 2 * cf - 3 * d) // 2
    n_kv = kvw // hd
    x_hd = xa_q_gain.shape[-1]
    x_heads = xa_wq.shape[-1] // x_hd
    n_groups = w_route_group.shape[-1]
    n_experts = w_route_expert.shape[-1]
    per_group = n_experts // n_groups
    assert 2 * hd == LANES and n_q % (2 * n_kv) == 0 and x_hd == LANES
    assert n_groups + n_experts <= LANES and s % ATTN_BLOCK == 0
    t = b * s
    tile_rows = EXPERT_TILE
    max_tiles = (t * TOP_K) // tile_rows + n_experts

    o_q = 3 * sc
    o_k = o_q + qw
    o_v = o_k + kvw
    o_cf = o_v + kvw
    o_g = o_cf + 2 * cf
    gq = jnp.concatenate([attn_q_gain, attn_q_gain], axis=-1) * (hd ** -0.5)
    gk = jnp.concatenate([attn_k_gain, attn_k_gain], axis=-1)

    for l in range(depth):
        wl = w_in[l].astype(BF16)
        wqkv = jnp.concatenate([wl[:, o_q:o_k], _dup_heads(wl[:, o_k:o_v], n_kv, hd),
                                _dup_heads(wl[:, o_v:o_cf], n_kv, hd)], axis=-1)
        gmix = _row(mix_norm[l])
        m1 = _sc_branch(x, gmix, wl[:, :o_q], sc_conv[l], w_out_sc[l].astype(BF16),
                        wl[:, o_g:o_g + d])
        m2 = _swa_branch(x, gmix, wqkv, _row(gq[l]), _row(gk[l]), attn_sinks[l], m1,
                         w_out_attn[l].astype(BF16), wl[:, o_g + d:o_g + 2 * d],
                         n_q=n_q, n_kv=n_kv, hd=hd)
        m3 = _cf_branch(x, gmix, wl[:, o_cf:o_g], cf_conv[l], _row(cf_conv_b[l]),
                        _row(cf_norm_g[l]), _row(cf_norm_b[l]), m2, w_out_cf[l].astype(BF16),
                        wl[:, o_g + 2 * d:o_g + 3 * d])
        kx, vx = _mem_kv(mem, _row(mem_norm[l]), xa_wkv[l].astype(BF16), _row(xa_k_gain[l]),
                         n_heads=x_heads, hd=x_hd)
        wr = jnp.concatenate(
            [w_route_group[l], w_route_expert[l],
             jnp.zeros((d, LANES - n_groups - n_experts), F32)], axis=-1).astype(BF16)
        br = jnp.concatenate(
            [b_route_group[l], b_route_expert[l],
             jnp.zeros((LANES - n_groups - n_experts,), F32)]).reshape(1, LANES)
        x2, hf, route, counts = _xattn_router(
            x, m3, w_mix_out[l].astype(BF16), _row(xa_norm[l]), xa_wq[l].astype(BF16),
            _row(xa_q_gain[l] * (x_hd ** -0.5)), kx, vx, xa_wo[l].astype(BF16),
            _row(ffn_norm[l]), wr, br, n_heads=x_heads, hd=x_hd, n_groups=n_groups,
            per_group=per_group)
        cnt = counts[0, n_groups:n_groups + n_experts].astype(I32)
        n_tile = (cnt + tile_rows - 1) // tile_rows
        tile_end = jnp.cumsum(n_tile)
        seg_start = (tile_end - n_tile) * tile_rows
        route2 = route.reshape(t, LANES)
        eid = route2[:, 0:TOP_K].astype(I32)
        rank = route2[:, TOP_K:2 * TOP_K].astype(I32)
        experts = jnp.arange(n_experts, dtype=I32)
        pos = (jnp.sum(jnp.where(eid[..., None] == experts, seg_start, 0), axis=-1)
               + rank).reshape(t * TOP_K)
        n_used = tile_end[-1:].astype(I32)
        tile_live = jnp.minimum(jnp.arange(max_tiles, dtype=I32), n_used - 1)
        tile_exp = jnp.minimum(jnp.sum(tile_end[None, :] <= tile_live[:, None], axis=1),
                               n_experts - 1).astype(I32)
        xs = _dispatch(tile_end * tile_rows, cnt, n_used, pos, hf.reshape(t, d // 2),
                       rows_out=max_tiles * tile_rows, tile_rows=tile_rows)
        later = jnp.where((experts[None, :] > experts[:, None]) & (n_tile[None, :] > 0),
                          experts[None, :], n_experts)
        next_exp = jnp.min(later, axis=1)
        next_exp = jnp.where(next_exp < n_experts, next_exp, -1).astype(I32)
        tile_next = jnp.sum(jnp.where(tile_exp[:, None] == experts, next_exp, 0), axis=1)
        ys = _experts(tile_exp, tile_next.astype(I32), n_used, xs, w_expert_gu, w_expert_down,
                      layer=l, tile_rows=tile_rows)
        x = _combine(pos, x2.reshape(t, d), route2, ys).reshape(b, s, d)
    return x
```

```python
import functools

import jax
import jax.numpy as jnp
from jax import lax
from jax.experimental import pallas as pl
from jax.experimental.pallas import tpu as pltpu

F32 = jnp.float32
BF16 = jnp.bfloat16
I32 = jnp.int32
U32 = jnp.uint32

EPS = 1e-6
NEG_INF = -1e30
ATTN_BLOCK = 128
TOP_K = 2
LANES = 128
SUBLANES = 8
TOKEN_TILE = 256
EXPERT_TILE = 256
VMEM_LIMIT = 56 * 1024 * 1024


def _const_spec(shape):
    nd = len(shape)
    return pl.BlockSpec(shape, lambda *_: (0,) * nd, pipeline_mode=pl.Buffered(1))


def _tile_spec(tm, width):
    return pl.BlockSpec((1, tm, width), lambda b, s: (b, s, 0))


def _params(n_axes):
    return pltpu.CompilerParams(dimension_semantics=("arbitrary",) * n_axes,
                                vmem_limit_bytes=VMEM_LIMIT)


def _rms(xf, gain):
    return xf * lax.rsqrt(jnp.mean(xf * xf, axis=-1, keepdims=True) + EPS) * gain


def _dot(a, b):
    return jnp.dot(a, b, preferred_element_type=F32)


def _dot_t(a, b):
    return lax.dot_general(a, b, (((1,), (1,)), ((), ())), preferred_element_type=F32)


def _pack_halves(v):
    w = v.shape[1] // 2
    hi = pltpu.bitcast(v[:, :w].astype(BF16).astype(F32), U32)
    lo = pltpu.bitcast(v[:, w:].astype(BF16).astype(F32), U32)
    return hi | (lo >> 16)


def _unpack_halves(u):
    return pltpu.bitcast(u & jnp.uint32(0xFFFF0000), F32), pltpu.bitcast(u << 16, F32)


def _zero_after(v, shape):
    bits = pltpu.bitcast(v[0:SUBLANES, 0:LANES], U32)
    zero = pltpu.bitcast((bits >> 16) >> 16, F32)
    return jnp.tile(zero, (shape[0] // SUBLANES, shape[1] // LANES))


def _halo_rows(taps):
    return -(-(taps - 1) // SUBLANES) * SUBLANES


def _conv_cols(ext_ref, w_ref, out_ref, tm, taps, cols, row_base=0):
    halo = _halo_rows(taps)
    rows = min(64, tm)
    groups = {}
    for k in range(taps):
        off = halo - (taps - 1) + k
        groups.setdefault(off % SUBLANES, []).append((off // SUBLANES, k))
    for r0 in range(row_base, row_base + tm, rows):
        total = None
        for mis in sorted(groups):
            members = groups[mis]
            span = SUBLANES * max(a for a, _ in members) + rows
            shifted = ext_ref[pl.ds(r0 + mis, span), cols]
            part = None
            for a, k in members:
                term = w_ref[pl.ds(k, 1), cols] * shifted[SUBLANES * a:SUBLANES * a + rows]
                part = term if part is None else part + term
            total = part if total is None else total + part
        out_ref[pl.ds(r0, rows), cols] = total


def _conv_carry(ext_ref, tm, taps):
    halo = _halo_rows(taps)
    ext_ref[0:halo, :] = ext_ref[tm:tm + halo, :]


def _sc_kernel(x_ref, g_ref, wsc_ref, cw_ref, wo_ref, wg_ref, out_ref, ext_ref, cv_ref, *, taps):
    tm = x_ref.shape[1]
    sc = cw_ref.shape[1]
    halo = _halo_rows(taps)

    @pl.when(pl.program_id(1) == 0)
    def _():
        ext_ref[0:halo, :] = jnp.zeros((halo, sc), F32)

    h = _rms(x_ref[0], g_ref[...]).astype(BF16)
    p = _dot(h, wsc_ref[...])
    ext_ref[halo:halo + tm, :] = p[:, sc:2 * sc] * p[:, 2 * sc:]

    def col_chunk(c, carry):
        _conv_cols(ext_ref, cw_ref, cv_ref, tm, taps, pl.ds(pl.multiple_of(c * LANES, LANES), LANES))
        return carry

    lax.fori_loop(0, sc // LANES, col_chunk, 0)
    _conv_carry(ext_ref, tm, taps)
    v = (p[:, :sc] * cv_ref[...]).astype(BF16)
    y = _dot(v, wo_ref[...])
    gate = jax.nn.sigmoid(_dot(h, wg_ref[...]))
    out_ref[0] = (gate * y).astype(BF16)


def _sc_branch(x, gain, wsc, cw, wo, wg):
    b, s, d = x.shape
    taps, sc = cw.shape
    tm = min(TOKEN_TILE, s)
    halo = _halo_rows(taps)
    return pl.pallas_call(
        functools.partial(_sc_kernel, taps=taps),
        grid=(b, s // tm),
        in_specs=[_tile_spec(tm, d), _const_spec(gain.shape), _const_spec(wsc.shape),
                  _const_spec(cw.shape), _const_spec(wo.shape), _const_spec(wg.shape)],
        out_specs=_tile_spec(tm, d),
        out_shape=jax.ShapeDtypeStruct((b, s, d), BF16),
        scratch_shapes=[pltpu.VMEM((halo + tm, sc), F32), pltpu.VMEM((tm, sc), F32)],
        compiler_params=_params(2),
        name="mixer_gated_conv",
    )(x, gain, wsc, cw, wo, wg)


def _swa_kernel(sink_ref, x_ref, g_ref, wqkv_ref, gq_ref, gk_ref, prev_ref, wo_ref, wg_ref,
                out_ref, kbuf, vbuf, attn_ref, *, n_q, n_kv, hd):
    tm = x_ref.shape[1]
    blk = ATTN_BLOCK
    grp = n_q // n_kv
    qw = n_q * hd
    kw = n_kv * LANES
    s_idx = pl.program_id(1)

    @pl.when(s_idx == 0)
    def _():
        kbuf[0:blk, :] = jnp.zeros((blk, kw), BF16)
        vbuf[0:blk, :] = jnp.zeros((blk, kw), BF16)

    h = _rms(x_ref[0], g_ref[...]).astype(BF16)
    qkv = _dot(h, wqkv_ref[...])
    for g in range(n_kv):
        ks = qkv[:, qw + g * LANES:qw + (g + 1) * LANES]
        ms = jnp.sum(ks * ks, axis=-1, keepdims=True) * (1.0 / LANES)
        kbuf[blk:blk + tm, g * LANES:(g + 1) * LANES] = (
            ks * lax.rsqrt(ms + EPS) * gk_ref[...]).astype(BF16)
    vbuf[blk:blk + tm, :] = qkv[:, qw + kw:].astype(BF16)

    lane = lax.broadcasted_iota(I32, (blk, LANES), 1)
    qi = lax.broadcasted_iota(I32, (blk, 2 * blk), 0)
    kj = lax.broadcasted_iota(I32, (blk, 2 * blk), 1)
    for jb in range(tm // blk):
        r0 = jb * blk
        first = jnp.logical_and(s_idx == 0, jb == 0)
        lo = jnp.where(first, blk, 0)
        valid = (kj >= jnp.maximum(qi + 1, lo)) & (kj <= qi + blk)
        for g in range(n_kv):
            kk = kbuf[r0:r0 + 2 * blk, g * LANES:(g + 1) * LANES]
            vv = vbuf[r0:r0 + 2 * blk, g * LANES:(g + 1) * LANES]
            qn = []
            for r in range(grp):
                head = g * grp + r
                slab, half = head // 2, head % 2
                qs = qkv[r0:r0 + blk, slab * LANES:(slab + 1) * LANES]
                qm = jnp.where(lane >= hd if half else lane < hd, qs, 0.0)
                ms = jnp.sum(qm * qm, axis=-1, keepdims=True) * (1.0 / hd)
                qn.append((qm * lax.rsqrt(ms + EPS) * gq_ref[...]).astype(BF16))
            scores = _dot_t(jnp.concatenate(qn, axis=0), kk)
            probs, dens = [], []
            for r in range(grp):
                sink = sink_ref[g * grp + r]
                sc = jnp.where(valid, scores[r * blk:(r + 1) * blk], NEG_INF)
                m = jnp.maximum(jnp.max(sc, axis=-1, keepdims=True), sink)
                p = jnp.exp(sc - m)
                dens.append(jnp.sum(p, axis=-1, keepdims=True) + jnp.exp(sink - m))
                probs.append(p.astype(BF16))
            o = _dot(jnp.concatenate(probs, axis=0), vv)
            outs = [o[r * blk:(r + 1) * blk] / dens[r] for r in range(grp)]
            for pr in range(grp // 2):
                slab = (g * grp) // 2 + pr
                attn_ref[r0:r0 + blk, slab * LANES:(slab + 1) * LANES] = jnp.where(
                    lane < hd, outs[2 * pr], outs[2 * pr + 1])

    kbuf[0:blk, :] = kbuf[tm:tm + blk, :]
    vbuf[0:blk, :] = vbuf[tm:tm + blk, :]
    y = _dot(attn_ref[...].astype(BF16), wo_ref[...])
    gate = jax.nn.sigmoid(_dot(h, wg_ref[...]))
    out_ref[0] = (prev_ref[0].astype(F32) + gate * y).astype(BF16)


def _swa_branch(x, gain, wqkv, gq, gk, sinks, prev, wo, wg, *, n_q, n_kv, hd):
    b, s, d = x.shape
    tm = min(TOKEN_TILE, s)
    kw = n_kv * LANES
    tile = lambda bb, ss, *_: (bb, ss, 0)
    const = lambda shape: pl.BlockSpec(shape, lambda *_: (0,) * len(shape),
                                       pipeline_mode=pl.Buffered(1))
    grid_spec = pltpu.PrefetchScalarGridSpec(
        num_scalar_prefetch=1,
        grid=(b, s // tm),
        in_specs=[pl.BlockSpec((1, tm, d), tile), const(gain.shape), const(wqkv.shape),
                  const(gq.shape), const(gk.shape), pl.BlockSpec((1, tm, d), tile),
                  const(wo.shape), const(wg.shape)],
        out_specs=pl.BlockSpec((1, tm, d), tile),
        scratch_shapes=[pltpu.VMEM((ATTN_BLOCK + tm, kw), BF16),
                        pltpu.VMEM((ATTN_BLOCK + tm, kw), BF16),
                        pltpu.VMEM((tm, n_q * hd), F32)],
    )
    return pl.pallas_call(
        functools.partial(_swa_kernel, n_q=n_q, n_kv=n_kv, hd=hd),
        grid_spec=grid_spec,
        out_shape=jax.ShapeDtypeStruct((b, s, d), BF16),
        compiler_params=_params(2),
        name="mixer_swa",
    )(sinks, x, gain, wqkv, gq, gk, prev, wo, wg)


def _cf_kernel(x_ref, g_ref, wcf_ref, cw_ref, cb_ref, lng_ref, lnb_ref, prev_ref, wo_ref, wg_ref,
               out_ref, ext_ref, cv_ref, h_ref, gate_ref, *, taps):
    tm, d = x_ref.shape[1:]
    cf = cw_ref.shape[1]
    halo = _halo_rows(taps)

    @pl.when(pl.program_id(1) == 0)
    def _():
        ext_ref[0:halo, :] = jnp.zeros((halo, cf), F32)

    h_ref[...] = _rms(x_ref[0], g_ref[...]).astype(BF16)
    width = 2 * LANES
    gwidth = d * width // cf
    for ci, c0 in enumerate(range(0, cf, width)):
        val = _dot(h_ref[...], wcf_ref[:, c0:c0 + width])
        gl = _dot(h_ref[...], wcf_ref[:, cf + c0:cf + c0 + width])
        gate = jax.nn.sigmoid(_dot(h_ref[...], wg_ref[:, ci * gwidth:(ci + 1) * gwidth]))
        gate_ref[:, ci * gwidth:(ci + 1) * gwidth] = gate
        ext_ref[halo:halo + tm, c0:c0 + width] = (
            val * jax.nn.sigmoid(gl) + _zero_after(gate, (tm, width)))
        for c in range(c0, c0 + width, LANES):
            _conv_cols(ext_ref, cw_ref, cv_ref, tm, taps, slice(c, c + LANES))
    _conv_carry(ext_ref, tm, taps)
    u = cv_ref[...] + cb_ref[...]
    xc = u - jnp.mean(u, axis=-1, keepdims=True)
    ln = xc * lax.rsqrt(jnp.mean(xc * xc, axis=-1, keepdims=True) + EPS) * lng_ref[...] + lnb_ref[...]
    act = (ln * jax.nn.sigmoid(ln)).astype(BF16)
    y = _dot(act, wo_ref[...])
    out_ref[0] = (prev_ref[0].astype(F32) + gate_ref[...] * y).astype(BF16)


def _cf_branch(x, gain, wcf, cw, cb, lng, lnb, prev, wo, wg):
    b, s, d = x.shape
    taps, cf = cw.shape
    tm = min(TOKEN_TILE, s)
    halo = _halo_rows(taps)
    return pl.pallas_call(
        functools.partial(_cf_kernel, taps=taps),
        grid=(b, s // tm),
        in_specs=[_tile_spec(tm, d), _const_spec(gain.shape), _const_spec(wcf.shape),
                  _const_spec(cw.shape), _const_spec(cb.shape), _const_spec(lng.shape),
                  _const_spec(lnb.shape), _tile_spec(tm, d), _const_spec(wo.shape),
                  _const_spec(wg.shape)],
        out_specs=_tile_spec(tm, d),
        out_shape=jax.ShapeDtypeStruct((b, s, d), BF16),
        scratch_shapes=[pltpu.VMEM((halo + tm, cf), F32), pltpu.VMEM((tm, cf), F32),
                        pltpu.VMEM((tm, d), BF16), pltpu.VMEM((tm, d), F32)],
        compiler_params=_params(2),
        name="mixer_conformer_conv",
    )(x, gain, wcf, cw, cb, lng, lnb, prev, wo, wg)


def _memkv_kernel(mem_ref, g_ref, wkv_ref, gk_ref, k_ref, v_ref, *, n_heads, hd):
    xw = n_heads * hd
    mn = _rms(mem_ref[0], g_ref[...]).astype(BF16)
    kv = _dot(mn, wkv_ref[...])
    for hh in range(n_heads):
        k_ref[0, :, hh * hd:(hh + 1) * hd] = _rms(kv[:, hh * hd:(hh + 1) * hd], gk_ref[...]).astype(BF16)
    v_ref[0] = kv[:, xw:].astype(BF16)


def _mem_kv(mem, gain, wkv, gk, *, n_heads, hd):
    b, m, d = mem.shape
    xw = n_heads * hd
    spec_m = pl.BlockSpec((1, m, d), lambda i: (i, 0, 0))
    spec_o = pl.BlockSpec((1, m, xw), lambda i: (i, 0, 0))
    return pl.pallas_call(
        functools.partial(_memkv_kernel, n_heads=n_heads, hd=hd),
        grid=(b,),
        in_specs=[spec_m, _const_spec(gain.shape), _const_spec(wkv.shape), _const_spec(gk.shape)],
        out_specs=[spec_o, spec_o],
        out_shape=[jax.ShapeDtypeStruct((b, m, xw), BF16)] * 2,
        compiler_params=_params(1),
        name="memory_kv",
    )(mem, gain, wkv, gk)


def _xattn_router_kernel(x_ref, mg_ref, wmix_ref, gx_ref, wq_ref, gq_ref, k_ref, v_ref, wo_ref,
                         gf_ref, wr_ref, br_ref, x_out, h_out, route_out, cnt_out, carry_ref,
                         *, n_heads, hd, n_groups, per_group, n_sub):
    @pl.when(jnp.logical_and(pl.program_id(0) == 0, pl.program_id(1) == 0))
    def _():
        carry_ref[...] = jnp.zeros_like(carry_ref)

    sub = x_ref.shape[1] // n_sub
    for i in range(n_sub):
        _xattn_router_rows(slice(i * sub, (i + 1) * sub), x_ref, mg_ref, wmix_ref, gx_ref, wq_ref,
                           gq_ref, k_ref, v_ref, wo_ref, gf_ref, wr_ref, br_ref, x_out, h_out,
                           route_out, carry_ref, n_heads=n_heads, hd=hd, n_groups=n_groups,
                           per_group=per_group)
    cnt_out[...] = carry_ref[...]


def _xattn_router_rows(rows, x_ref, mg_ref, wmix_ref, gx_ref, wq_ref, gq_ref, k_ref, v_ref, wo_ref,
                       gf_ref, wr_ref, br_ref, x_out, h_out, route_out, carry_ref,
                       *, n_heads, hd, n_groups, per_group):
    tm = rows.stop - rows.start
    x1 = x_ref[0, rows, :] + _dot(mg_ref[0, rows, :], wmix_ref[...])
    q = _dot(_rms(x1, gx_ref[...]).astype(BF16), wq_ref[...])
    outs = []
    for hh in range(n_heads):
        qn = (_rms(q[:, hh * hd:(hh + 1) * hd], gq_ref[...])).astype(BF16)
        sc = _dot_t(qn, k_ref[0, :, hh * hd:(hh + 1) * hd])
        p = jnp.exp(sc - jnp.max(sc, axis=-1, keepdims=True))
        den = jnp.sum(p, axis=-1, keepdims=True)
        outs.append(_dot(p.astype(BF16), v_ref[0, :, hh * hd:(hh + 1) * hd]) / den)
    x2 = x1 + _dot(jnp.concatenate(outs, axis=-1).astype(BF16), wo_ref[...])
    x_out[0, rows, :] = x2
    hf = _rms(x2, gf_ref[...])
    h_out[0, rows, :] = _pack_halves(hf)
    n_experts = n_groups * per_group
    lg = _dot_t(wr_ref[...], hf.astype(BF16)) + br_ref[:, 0:1]
    sub = lax.broadcasted_iota(I32, (SUBLANES, tm), 0).astype(F32)

    def first_row(mask):
        return jnp.min(jnp.where(mask, sub, float(SUBLANES)), axis=0, keepdims=True)

    gmask = sub < n_groups
    gl = jnp.where(gmask, lg[0:SUBLANES], NEG_INF)
    ge = jnp.where(gmask, jnp.exp(gl - jnp.max(gl, axis=0, keepdims=True)), 0.0)
    gsum = jnp.sum(ge, axis=0, keepdims=True)
    pg_top = 1.0 / gsum
    g_idx = first_row((ge / gsum == pg_top) & gmask)
    el = lg[SUBLANES:2 * SUBLANES]
    for g in range(1, n_groups):
        el = jnp.where(g_idx == g, lg[(g + 1) * SUBLANES:(g + 2) * SUBLANES], el)
    ee = jnp.exp(el - jnp.max(el, axis=0, keepdims=True))
    esum = jnp.sum(ee, axis=0, keepdims=True)
    pe = ee / esum
    p1 = 1.0 / esum
    i1 = first_row(pe == p1)
    pe2 = jnp.where(sub == i1, -1.0, pe)
    p2 = jnp.max(pe2, axis=0, keepdims=True)
    i2 = first_row((pe2 == p2) & (sub != i1))
    den = p1 + p2
    w1 = pg_top * (p1 / den)
    w2 = pg_top * (p2 / den)
    e1 = g_idx * per_group + i1
    e2 = g_idx * per_group + i2
    erow = lax.broadcasted_iota(I32, (n_experts, tm), 0).astype(F32)
    hit1, hit2 = erow == e1, erow == e2
    assign = jnp.where(hit1 | hit2, 1.0, 0.0)
    tj = lax.broadcasted_iota(I32, (tm, tm), 0)
    tt = lax.broadcasted_iota(I32, (tm, tm), 1)
    earlier = jnp.where(tj < tt, 1.0, 0.0).astype(BF16)
    before = _dot(assign.astype(BF16), earlier) + carry_ref[:, 0:1]
    rank1 = jnp.sum(jnp.where(hit1, before, 0.0), axis=0, keepdims=True)
    rank2 = jnp.sum(jnp.where(hit2, before, 0.0), axis=0, keepdims=True)
    carry_ref[...] = carry_ref[...] + jnp.sum(assign, axis=1, keepdims=True)
    packed = jnp.zeros((SUBLANES, tm), F32)
    for idx, val in enumerate([e1, e2, rank1, rank2, w1, w2]):
        packed = jnp.where(sub == idx, val, packed)
    route_out[0, :, rows] = packed


def _xattn_router(x, merged, wmix, gx, wq, gq, kx, vx, wo, gf, wr, br, *, n_heads, hd, n_groups,
                  per_group):
    b, s, d = x.shape
    m, xw = kx.shape[1:]
    tm, n_sub = min(TOKEN_TILE, s), 1
    n_experts = n_groups * per_group
    assert per_group == SUBLANES and n_groups <= SUBLANES
    kv_spec = pl.BlockSpec((1, m, xw), lambda bb, ss: (bb, 0, 0))
    return pl.pallas_call(
        functools.partial(_xattn_router_kernel, n_heads=n_heads, hd=hd, n_groups=n_groups,
                          per_group=per_group, n_sub=n_sub),
        grid=(b, s // tm),
        in_specs=[_tile_spec(tm, d), _tile_spec(tm, d), _const_spec(wmix.shape),
                  _const_spec(gx.shape), _const_spec(wq.shape), _const_spec(gq.shape),
                  kv_spec, kv_spec, _const_spec(wo.shape), _const_spec(gf.shape),
                  _const_spec(wr.shape), _const_spec(br.shape)],
        out_specs=[_tile_spec(tm, d), _tile_spec(tm, d // 2),
                   pl.BlockSpec((1, SUBLANES, tm), lambda bb, ss: (bb, 0, ss)),
                   pl.BlockSpec((n_experts, LANES), lambda bb, ss: (0, 0))],
        out_shape=[jax.ShapeDtypeStruct((b, s, d), F32), jax.ShapeDtypeStruct((b, s, d // 2), U32),
                   jax.ShapeDtypeStruct((b, SUBLANES, s), F32),
                   jax.ShapeDtypeStruct((n_experts, LANES), F32)],
        scratch_shapes=[pltpu.VMEM((n_experts, LANES), F32)],
        compiler_params=_params(2),
        name="mixout_xattn_router",
    )(x, merged, wmix, gx, wq, gq, kx, vx, wo, gf, wr, br)


def _dispatch_kernel(seg_end_ref, cnt_ref, used_ref, pos_ref, h_ref, xs_ref, zero_ref, sems, *,
                     n_experts, tile_rows):
    step = pl.program_id(0)
    tokens = pos_ref.shape[0] // TOP_K
    max_tiles = xs_ref.shape[0] // tile_rows

    def fill_copy(row_start):
        start = pl.multiple_of(row_start, tile_rows)
        return pltpu.make_async_copy(zero_ref, xs_ref.at[pl.ds(start, tile_rows)], sems.at[0])

    @pl.when(step == 0)
    def _():
        zero_ref[...] = jnp.zeros_like(zero_ref)
        for e in range(n_experts):
            @pl.when(cnt_ref[e] > 0)
            def _():
                fill_copy(seg_end_ref[e] - tile_rows).start()

        def start_tail(i, c):
            fill_copy(i * tile_rows).start()
            return c

        def wait_tail(i, c):
            fill_copy(i * tile_rows).wait()
            return c

        lax.fori_loop(used_ref[0], max_tiles, start_tail, 0)
        for e in range(n_experts):
            @pl.when(cnt_ref[e] > 0)
            def _():
                fill_copy(seg_end_ref[e] - tile_rows).wait()
        lax.fori_loop(used_ref[0], max_tiles, wait_tail, 0)

    def row_copy(r, k):
        return pltpu.make_async_copy(h_ref.at[pl.ds(r, 1)],
                                     xs_ref.at[pl.ds(pos_ref[TOP_K * r + k], 1)], sems.at[1 + k])

    for r in range(tokens):
        for k in range(TOP_K):
            row_copy(r, k).start(priority=k)
    for k in range(TOP_K):
        for r in range(tokens):
            row_copy(r, k).wait()


def _dispatch(seg_end, cnt, n_used, pos, h, *, rows_out, tile_rows):
    t, w = h.shape
    tokens = min(TOKEN_TILE, t)
    grid_spec = pltpu.PrefetchScalarGridSpec(
        num_scalar_prefetch=3,
        grid=(t // tokens,),
        in_specs=[pl.BlockSpec((tokens * TOP_K,), lambda i, *_: (i,), memory_space=pltpu.SMEM),
                  pl.BlockSpec((tokens, w), lambda i, *_: (i, 0))],
        out_specs=pl.BlockSpec(memory_space=pl.ANY),
        scratch_shapes=[pltpu.VMEM((tile_rows, w), h.dtype),
                        pltpu.SemaphoreType.DMA((1 + TOP_K,))],
    )
    return pl.pallas_call(
        functools.partial(_dispatch_kernel, n_experts=cnt.shape[0], tile_rows=tile_rows),
        grid_spec=grid_spec,
        out_shape=jax.ShapeDtypeStruct((rows_out, w), h.dtype),
        compiler_params=_params(1),
        name="moe_dispatch",
    )(seg_end, cnt, n_used, pos, h)


def _expert_kernel(exp_ref, next_ref, used_ref, xs_ref, wgu_hbm, wdn_hbm, ys_ref, gu_stage, dn_stage,
                   wgu_bf, wdn_bf, sems, *, layer):
    de = wdn_bf.shape[0]
    i = pl.program_id(0)
    live = i < used_ref[0]
    expert = exp_ref[i]
    new_expert = jnp.logical_or(i == 0, expert != exp_ref[jnp.maximum(i - 1, 0)])

    def fetch(e):
        return (pltpu.make_async_copy(wgu_hbm.at[layer, e], gu_stage, sems.at[0]),
                pltpu.make_async_copy(wdn_hbm.at[layer, e], dn_stage, sems.at[1]))

    @pl.when(jnp.logical_and(live, new_expert))
    def _():
        @pl.when(i == 0)
        def _():
            for cp in fetch(expert):
                cp.start()
        for cp in fetch(expert):
            cp.wait()
        wgu_bf[...] = gu_stage[...].astype(BF16)
        wdn_bf[...] = dn_stage[...].astype(BF16)

        @pl.when(next_ref[i] >= 0)
        def _():
            for cp in fetch(next_ref[i]):
                cp.start()

    @pl.when(live)
    def _():
        hi, lo = _unpack_halves(xs_ref[...])
        xb = jnp.concatenate([hi, lo], axis=1).astype(BF16)
        gu = _dot(xb, wgu_bf[...])
        act = (jax.nn.silu(gu[:, :de]) * gu[:, de:]).astype(BF16)
        ys_ref[...] = _pack_halves(_dot(act, wdn_bf[...]))

    @pl.when(jnp.logical_not(live))
    def _():
        ys_ref[...] = jnp.zeros_like(ys_ref)


def _experts(tile_exp, tile_next, n_used, xs, wgu, wdn, *, layer, tile_rows):
    rows, w = xs.shape
    d = 2 * w
    de = wdn.shape[2]
    grid_spec = pltpu.PrefetchScalarGridSpec(
        num_scalar_prefetch=3,
        grid=(rows // tile_rows,),
        in_specs=[pl.BlockSpec((tile_rows, w), lambda i, *_: (i, 0)),
                  pl.BlockSpec(memory_space=pl.ANY), pl.BlockSpec(memory_space=pl.ANY)],
        out_specs=pl.BlockSpec((tile_rows, w), lambda i, *_: (i, 0)),
        scratch_shapes=[pltpu.VMEM((d, 2 * de), F32), pltpu.VMEM((de, d), F32),
                        pltpu.VMEM((d, 2 * de), BF16), pltpu.VMEM((de, d), BF16),
                        pltpu.SemaphoreType.DMA((2,))],
    )
    return pl.pallas_call(
        functools.partial(_expert_kernel, layer=layer),
        grid_spec=grid_spec,
        out_shape=jax.ShapeDtypeStruct((rows, w), U32),
        compiler_params=_params(1),
        name="moe_experts",
    )(tile_exp, tile_next, n_used, xs, wgu, wdn)


def _combine_kernel(pos_ref, x_ref, route_ref, ys_ref, out_ref, buf, sems):
    tokens = x_ref.shape[0]

    def row_copy(r, k):
        return pltpu.make_async_copy(ys_ref.at[pl.ds(pos_ref[TOP_K * r + k], 1)],
                                     buf.at[k, pl.ds(r, 1)], sems.at[k])

    for r in range(tokens):
        for k in range(TOP_K):
            row_copy(r, k).start(priority=k)
    for k in range(TOP_K):
        for r in range(tokens):
            row_copy(r, k).wait()
    w = x_ref.shape[1] // 2
    acc_hi, acc_lo = x_ref[:, :w], x_ref[:, w:]
    for k in range(TOP_K):
        hi, lo = _unpack_halves(buf[k])
        weight = route_ref[:, k:k + 1]
        acc_hi = acc_hi + weight * hi
        acc_lo = acc_lo + weight * lo
    out_ref[:, :w] = acc_hi
    out_ref[:, w:] = acc_lo


def _combine(pos, x, route, ys):
    t, d = x.shape
    tokens = min(TOKEN_TILE, t)
    return pl.pallas_call(
        _combine_kernel,
        grid=(t // tokens,),
        in_specs=[pl.BlockSpec((tokens * TOP_K,), lambda i: (i,), memory_space=pltpu.SMEM),
                  pl.BlockSpec((tokens, d), lambda i: (i, 0)),
                  pl.BlockSpec((tokens, TOP_K), lambda i: (i, 0)),
                  pl.BlockSpec(memory_space=pl.ANY)],
        out_specs=pl.BlockSpec((tokens, d), lambda i: (i, 0)),
        out_shape=jax.ShapeDtypeStruct((t, d), F32),
        scratch_shapes=[pltpu.VMEM((TOP_K, tokens, d // 2), U32),
                        pltpu.SemaphoreType.DMA((TOP_K,))],
        compiler_params=_params(1),
        name="moe_combine",
    )(pos, x, route, ys)


def _dup_heads(w, n_heads, hd):
    d = w.shape[0]
    w3 = w.reshape(d, n_heads, hd)
    return jnp.concatenate([w3, w3], axis=-1).reshape(d, n_heads * 2 * hd)


def _row(v):
    return v.reshape(1, -1).astype(F32)


def kernel(x, mem, mix_norm, w_in, sc_conv, attn_q_gain, attn_k_gain, attn_sinks, w_out_sc, w_out_attn, cf_conv, cf_conv_b, cf_norm_g, cf_norm_b, w_out_cf, w_mix_out, xa_norm, mem_norm, xa_wq, xa_wkv, xa_q_gain, xa_k_gain, xa_wo, ffn_norm, w_route_group, b_route_group, w_route_expert, b_route_expert, w_expert_gu, w_expert_down):
    b, s, d = x.shape
    depth = w_in.shape[0]
    sc = sc_conv.shape[-1]
    cf = cf_conv.shape[-1]
    hd = attn_q_gain.shape[-1]
    n_q = attn_sinks.shape[-1]
    qw = n_q * hd
    kvw = (w_in.shape[-1] - 3 * sc - qw - 2 * cf - 3 * d) // 2
    n_kv = kvw // hd
    x_hd = xa_q_gain.shape[-1]
    x_heads = xa_wq.shape[-1] // x_hd
    n_groups = w_route_group.shape[-1]
    n_experts = w_route_expert.shape[-1]
    per_group = n_experts // n_groups
    assert 2 * hd == LANES and n_q % (2 * n_kv) == 0 and x_hd == LANES
    assert s % ATTN_BLOCK == 0
    t = b * s
    tile_rows = EXPERT_TILE
    max_tiles = (t * TOP_K) // tile_rows + n_experts

    o_q = 3 * sc
    o_k = o_q + qw
    o_v = o_k + kvw
    o_cf = o_v + kvw
    o_g = o_cf + 2 * cf
    gq = jnp.concatenate([attn_q_gain, attn_q_gain], axis=-1) * (hd ** -0.5)
    gk = jnp.concatenate([attn_k_gain, attn_k_gain], axis=-1)

    for l in range(depth):
        wl = w_in[l].astype(BF16)
        wqkv = jnp.concatenate([wl[:, o_q:o_k], _dup_heads(wl[:, o_k:o_v], n_kv, hd),
                                _dup_heads(wl[:, o_v:o_cf], n_kv, hd)], axis=-1)
        gmix = _row(mix_norm[l])
        m1 = _sc_branch(x, gmix, wl[:, :o_q], sc_conv[l], w_out_sc[l].astype(BF16),
                        wl[:, o_g:o_g + d])
        m2 = _swa_branch(x, gmix, wqkv, _row(gq[l]), _row(gk[l]), attn_sinks[l], m1,
                         w_out_attn[l].astype(BF16), wl[:, o_g + d:o_g + 2 * d],
                         n_q=n_q, n_kv=n_kv, hd=hd)
        m3 = _cf_branch(x, gmix, wl[:, o_cf:o_g], cf_conv[l], _row(cf_conv_b[l]),
                        _row(cf_norm_g[l]), _row(cf_norm_b[l]), m2, w_out_cf[l].astype(BF16),
                        wl[:, o_g + 2 * d:o_g + 3 * d])
        kx, vx = _mem_kv(mem, _row(mem_norm[l]), xa_wkv[l].astype(BF16), _row(xa_k_gain[l]),
                         n_heads=x_heads, hd=x_hd)
        pad_g = SUBLANES - n_groups
        pad_e = -(SUBLANES + n_experts) % (2 * SUBLANES)
        wr = jnp.concatenate(
            [w_route_group[l].T, jnp.zeros((pad_g, d), F32), w_route_expert[l].T,
             jnp.zeros((pad_e, d), F32)], axis=0).astype(BF16)
        br = jnp.broadcast_to(jnp.concatenate(
            [b_route_group[l], jnp.zeros((pad_g,), F32), b_route_expert[l],
             jnp.zeros((pad_e,), F32)])[:, None], (wr.shape[0], LANES))
        x2, hf, route, counts = _xattn_router(
            x, m3, w_mix_out[l].astype(BF16), _row(xa_norm[l]), xa_wq[l].astype(BF16),
            _row(xa_q_gain[l] * (x_hd ** -0.5)), kx, vx, xa_wo[l].astype(BF16),
            _row(ffn_norm[l]), wr, br, n_heads=x_heads, hd=x_hd, n_groups=n_groups,
            per_group=per_group)
        cnt = counts[:, 0].astype(I32)
        n_tile = (cnt + tile_rows - 1) // tile_rows
        tile_end = jnp.cumsum(n_tile)
        seg_start = (tile_end - n_tile) * tile_rows
        route2 = jnp.swapaxes(route, 1, 2).reshape(t, SUBLANES)
        eid = route2[:, 0:TOP_K].astype(I32)
        rank = route2[:, TOP_K:2 * TOP_K].astype(I32)
        gate_w = route2[:, 2 * TOP_K:3 * TOP_K]
        experts = jnp.arange(n_experts, dtype=I32)
        pos = (jnp.sum(jnp.where(eid[..., None] == experts, seg_start, 0), axis=-1)
               + rank).reshape(t * TOP_K)
        n_used = tile_end[-1:].astype(I32)
        tile_live = jnp.minimum(jnp.arange(max_tiles, dtype=I32), n_used - 1)
        tile_exp = jnp.minimum(jnp.sum(tile_end[None, :] <= tile_live[:, None], axis=1),
                               n_experts - 1).astype(I32)
        xs = _dispatch(tile_end * tile_rows, cnt, n_used, pos, hf.reshape(t, d // 2),
                       rows_out=max_tiles * tile_rows, tile_rows=tile_rows)
        later = jnp.where((experts[None, :] > experts[:, None]) & (n_tile[None, :] > 0),
                          experts[None, :], n_experts)
        next_exp = jnp.min(later, axis=1)
        next_exp = jnp.where(next_exp < n_experts, next_exp, -1).astype(I32)
        tile_next = jnp.sum(jnp.where(tile_exp[:, None] == experts, next_exp, 0), axis=1)
        ys = _experts(tile_exp, tile_next.astype(I32), n_used, xs, w_expert_gu, w_expert_down,
                      layer=l, tile_rows=tile_rows)
        x = _combine(pos, x2.reshape(t, d), gate_w, ys).reshape(b, s, d)
    return x
```

```python
import functools

import jax
import jax.numpy as jnp
from jax import lax
from jax.experimental import pallas as pl
from jax.experimental.pallas import tpu as pltpu

F32 = jnp.float32
BF16 = jnp.bfloat16
I32 = jnp.int32
U32 = jnp.uint32

EPS = 1e-6
NEG_INF = -1e30
ATTN_BLOCK = 128
TOP_K = 2
LANES = 128
SUBLANES = 8
TOKEN_TILE = 256
EXPERT_TILE = 256
ROUTE_TILE = 512
VMEM_LIMIT = 56 * 1024 * 1024


def _const_spec(shape):
    nd = len(shape)
    return pl.BlockSpec(shape, lambda *_: (0,) * nd, pipeline_mode=pl.Buffered(1))


def _tile_spec(tm, width):
    return pl.BlockSpec((1, tm, width), lambda b, s: (b, s, 0))


def _params(n_axes):
    return pltpu.CompilerParams(dimension_semantics=("arbitrary",) * n_axes,
                                vmem_limit_bytes=VMEM_LIMIT)


def _rms(xf, gain):
    return xf * lax.rsqrt(jnp.mean(xf * xf, axis=-1, keepdims=True) + EPS) * gain


def _dot(a, b):
    return jnp.dot(a, b, preferred_element_type=F32)


def _dot_t(a, b):
    return lax.dot_general(a, b, (((1,), (1,)), ((), ())), preferred_element_type=F32)


def _pack_halves(v):
    w = v.shape[1] // 2
    hi = pltpu.bitcast(v[:, :w].astype(BF16).astype(F32), U32)
    lo = pltpu.bitcast(v[:, w:].astype(BF16).astype(F32), U32)
    return hi | (lo >> 16)


def _unpack_halves(u):
    return pltpu.bitcast(u & jnp.uint32(0xFFFF0000), F32), pltpu.bitcast(u << 16, F32)


def _zero_after(v, shape):
    bits = pltpu.bitcast(v[0:SUBLANES, 0:LANES], U32)
    zero = pltpu.bitcast((bits >> 16) >> 16, F32)
    return jnp.tile(zero, (shape[0] // SUBLANES, shape[1] // LANES))


def _halo_rows(taps):
    return -(-(taps - 1) // SUBLANES) * SUBLANES


def _conv_cols(ext_ref, w_ref, out_ref, tm, taps, cols, row_base=0):
    halo = _halo_rows(taps)
    rows = min(64, tm)
    groups = {}
    for k in range(taps):
        off = halo - (taps - 1) + k
        groups.setdefault(off % SUBLANES, []).append((off // SUBLANES, k))
    for r0 in range(row_base, row_base + tm, rows):
        total = None
        for mis in sorted(groups):
            members = groups[mis]
            span = SUBLANES * max(a for a, _ in members) + rows
            shifted = ext_ref[pl.ds(r0 + mis, span), cols]
            part = None
            for a, k in members:
                term = w_ref[pl.ds(k, 1), cols] * shifted[SUBLANES * a:SUBLANES * a + rows]
                part = term if part is None else part + term
            total = part if total is None else total + part
        out_ref[pl.ds(r0, rows), cols] = total


def _conv_carry(ext_ref, tm, taps):
    halo = _halo_rows(taps)
    ext_ref[0:halo, :] = ext_ref[tm:tm + halo, :]


def _sc_kernel(x_ref, g_ref, wsc_ref, cw_ref, wo_ref, wg_ref, out_ref, h_out, ext_ref, cv_ref, *,
               taps):
    tm = x_ref.shape[1]
    sc = cw_ref.shape[1]
    halo = _halo_rows(taps)

    @pl.when(pl.program_id(1) == 0)
    def _():
        ext_ref[0:halo, :] = jnp.zeros((halo, sc), F32)

    h = _rms(x_ref[0], g_ref[...]).astype(BF16)
    h_out[0] = h
    p = _dot(h, wsc_ref[...])
    ext_ref[halo:halo + tm, :] = p[:, sc:2 * sc] * p[:, 2 * sc:]

    def col_chunk(c, carry):
        _conv_cols(ext_ref, cw_ref, cv_ref, tm, taps, pl.ds(pl.multiple_of(c * LANES, LANES), LANES))
        return carry

    lax.fori_loop(0, sc // LANES, col_chunk, 0)
    _conv_carry(ext_ref, tm, taps)
    v = (p[:, :sc] * cv_ref[...]).astype(BF16)
    y = _dot(v, wo_ref[...])
    gate = jax.nn.sigmoid(_dot(h, wg_ref[...]))
    out_ref[0] = (gate * y).astype(BF16)


def _sc_branch(x, gain, wsc, cw, wo, wg):
    b, s, d = x.shape
    taps, sc = cw.shape
    tm = min(TOKEN_TILE, s)
    halo = _halo_rows(taps)
    return pl.pallas_call(
        functools.partial(_sc_kernel, taps=taps),
        grid=(b, s // tm),
        in_specs=[_tile_spec(tm, d), _const_spec(gain.shape), _const_spec(wsc.shape),
                  _const_spec(cw.shape), _const_spec(wo.shape), _const_spec(wg.shape)],
        out_specs=[_tile_spec(tm, d), _tile_spec(tm, d)],
        out_shape=[jax.ShapeDtypeStruct((b, s, d), BF16)] * 2,
        scratch_shapes=[pltpu.VMEM((halo + tm, sc), F32), pltpu.VMEM((tm, sc), F32)],
        compiler_params=_params(2),
        name="mixer_gated_conv",
    )(x, gain, wsc, cw, wo, wg)


def _swa_kernel(sink_ref, h_ref, wqkv_ref, gq_ref, gk_ref, prev_ref, wo_ref, wg_ref,
                out_ref, kbuf, vbuf, attn_ref, *, n_q, n_kv, hd):
    tm = h_ref.shape[1]
    blk = ATTN_BLOCK
    grp = n_q // n_kv
    qw = n_q * hd
    kw = n_kv * LANES
    s_idx = pl.program_id(1)

    @pl.when(s_idx == 0)
    def _():
        kbuf[0:blk, :] = jnp.zeros((blk, kw), BF16)
        vbuf[0:blk, :] = jnp.zeros((blk, kw), BF16)

    h = h_ref[0]
    qkv = _dot(h, wqkv_ref[...])
    for g in range(n_kv):
        ks = qkv[:, qw + g * LANES:qw + (g + 1) * LANES]
        ms = jnp.sum(ks * ks, axis=-1, keepdims=True) * (1.0 / LANES)
        kbuf[blk:blk + tm, g * LANES:(g + 1) * LANES] = (
            ks * lax.rsqrt(ms + EPS) * gk_ref[...]).astype(BF16)
    vbuf[blk:blk + tm, :] = qkv[:, qw + kw:].astype(BF16)

    lane = lax.broadcasted_iota(I32, (blk, LANES), 1)
    qi = lax.broadcasted_iota(I32, (blk, 2 * blk), 0)
    kj = lax.broadcasted_iota(I32, (blk, 2 * blk), 1)
    for jb in range(tm // blk):
        r0 = jb * blk
        first = jnp.logical_and(s_idx == 0, jb == 0)
        lo = jnp.where(first, blk, 0)
        valid = (kj >= jnp.maximum(qi + 1, lo)) & (kj <= qi + blk)
        for g in range(n_kv):
            kk = kbuf[r0:r0 + 2 * blk, g * LANES:(g + 1) * LANES]
            vv = vbuf[r0:r0 + 2 * blk, g * LANES:(g + 1) * LANES]
            qn = []
            for r in range(grp):
                head = g * grp + r
                slab, half = head // 2, head % 2
                qs = qkv[r0:r0 + blk, slab * LANES:(slab + 1) * LANES]
                qm = jnp.where(lane >= hd if half else lane < hd, qs, 0.0)
                ms = jnp.sum(qm * qm, axis=-1, keepdims=True) * (1.0 / hd)
                qn.append((qm * lax.rsqrt(ms + EPS) * gq_ref[...]).astype(BF16))
            scores = _dot_t(jnp.concatenate(qn, axis=0), kk)
            probs, dens = [], []
            for r in range(grp):
                sink = sink_ref[g * grp + r]
                sc = jnp.where(valid, scores[r * blk:(r + 1) * blk], NEG_INF)
                m = jnp.maximum(jnp.max(sc, axis=-1, keepdims=True), sink)
                p = jnp.exp(sc - m)
                dens.append(jnp.sum(p, axis=-1, keepdims=True) + jnp.exp(sink - m))
                probs.append(p.astype(BF16))
            o = _dot(jnp.concatenate(probs, axis=0), vv)
            outs = [o[r * blk:(r + 1) * blk] / dens[r] for r in range(grp)]
            for pr in range(grp // 2):
                slab = (g * grp) // 2 + pr
                attn_ref[r0:r0 + blk, slab * LANES:(slab + 1) * LANES] = jnp.where(
                    lane < hd, outs[2 * pr], outs[2 * pr + 1])

    kbuf[0:blk, :] = kbuf[tm:tm + blk, :]
    vbuf[0:blk, :] = vbuf[tm:tm + blk, :]
    y = _dot(attn_ref[...].astype(BF16), wo_ref[...])
    gate = jax.nn.sigmoid(_dot(h, wg_ref[...]))
    out_ref[0] = (prev_ref[0].astype(F32) + gate * y).astype(BF16)


def _swa_branch(h, wqkv, gq, gk, sinks, prev, wo, wg, *, n_q, n_kv, hd):
    b, s, d = h.shape
    tm = min(TOKEN_TILE, s)
    kw = n_kv * LANES
    tile = lambda bb, ss, *_: (bb, ss, 0)
    const = lambda shape: pl.BlockSpec(shape, lambda *_: (0,) * len(shape),
                                       pipeline_mode=pl.Buffered(1))
    grid_spec = pltpu.PrefetchScalarGridSpec(
        num_scalar_prefetch=1,
        grid=(b, s // tm),
        in_specs=[pl.BlockSpec((1, tm, d), tile), const(wqkv.shape),
                  const(gq.shape), const(gk.shape), pl.BlockSpec((1, tm, d), tile),
                  const(wo.shape), const(wg.shape)],
        out_specs=pl.BlockSpec((1, tm, d), tile),
        scratch_shapes=[pltpu.VMEM((ATTN_BLOCK + tm, kw), BF16),
                        pltpu.VMEM((ATTN_BLOCK + tm, kw), BF16),
                        pltpu.VMEM((tm, n_q * hd), F32)],
    )
    return pl.pallas_call(
        functools.partial(_swa_kernel, n_q=n_q, n_kv=n_kv, hd=hd),
        grid_spec=grid_spec,
        out_shape=jax.ShapeDtypeStruct((b, s, d), BF16),
        compiler_params=_params(2),
        name="mixer_swa",
    )(sinks, h, wqkv, gq, gk, prev, wo, wg)


def _cf_kernel(hin_ref, wcf_ref, cw_ref, cb_ref, lng_ref, lnb_ref, prev_ref, wo_ref, wg_ref,
               out_ref, ext_ref, cv_ref, gate_ref, *, taps):
    tm, d = hin_ref.shape[1:]
    cf = cw_ref.shape[1]
    halo = _halo_rows(taps)
    h_ref = hin_ref.at[0]

    @pl.when(pl.program_id(1) == 0)
    def _():
        ext_ref[0:halo, :] = jnp.zeros((halo, cf), F32)

    width = 2 * LANES
    gwidth = d * width // cf
    gate = None
    for ci, c0 in enumerate(range(0, cf, width)):
        val = _dot(h_ref[...], wcf_ref[:, c0:c0 + width])
        gl = _dot(h_ref[...], wcf_ref[:, cf + c0:cf + c0 + width])
        u_chunk = val * jax.nn.sigmoid(gl)
        if gate is not None:
            u_chunk = u_chunk + _zero_after(gate, (tm, width))
        ext_ref[halo:halo + tm, c0:c0 + width] = u_chunk
        gate = jax.nn.sigmoid(_dot(h_ref[...], wg_ref[:, ci * gwidth:(ci + 1) * gwidth]))
        gate_ref[:, ci * gwidth:(ci + 1) * gwidth] = gate
        for c in range(c0, c0 + width, LANES):
            _conv_cols(ext_ref, cw_ref, cv_ref, tm, taps, slice(c, c + LANES))
    _conv_carry(ext_ref, tm, taps)
    u = cv_ref[...] + cb_ref[...] + _zero_after(gate, (tm, cf))
    xc = u - jnp.mean(u, axis=-1, keepdims=True)
    ln = xc * lax.rsqrt(jnp.mean(xc * xc, axis=-1, keepdims=True) + EPS) * lng_ref[...] + lnb_ref[...]
    act = (ln * jax.nn.sigmoid(ln)).astype(BF16)
    y = _dot(act, wo_ref[...])
    out_ref[0] = (prev_ref[0].astype(F32) + gate_ref[...] * y).astype(BF16)


def _cf_branch(h, wcf, cw, cb, lng, lnb, prev, wo, wg):
    b, s, d = h.shape
    taps, cf = cw.shape
    tm = min(TOKEN_TILE, s)
    halo = _halo_rows(taps)
    return pl.pallas_call(
        functools.partial(_cf_kernel, taps=taps),
        grid=(b, s // tm),
        in_specs=[_tile_spec(tm, d), _const_spec(wcf.shape),
                  _const_spec(cw.shape), _const_spec(cb.shape), _const_spec(lng.shape),
                  _const_spec(lnb.shape), _tile_spec(tm, d), _const_spec(wo.shape),
                  _const_spec(wg.shape)],
        out_specs=_tile_spec(tm, d),
        out_shape=jax.ShapeDtypeStruct((b, s, d), BF16),
        scratch_shapes=[pltpu.VMEM((halo + tm, cf), F32), pltpu.VMEM((tm, cf), F32),
                        pltpu.VMEM((tm, d), F32)],
        compiler_params=_params(2),
        name="mixer_conformer_conv",
    )(h, wcf, cw, cb, lng, lnb, prev, wo, wg)


def _memkv_kernel(mem_ref, g_ref, wkv_ref, gk_ref, k_ref, v_ref, *, n_heads, hd):
    xw = n_heads * hd
    mn = _rms(mem_ref[0], g_ref[...]).astype(BF16)
    kv = _dot(mn, wkv_ref[...])
    for hh in range(n_heads):
        k_ref[0, :, hh * hd:(hh + 1) * hd] = _rms(kv[:, hh * hd:(hh + 1) * hd], gk_ref[...]).astype(BF16)
    v_ref[0] = kv[:, xw:].astype(BF16)


def _mem_kv(mem, gain, wkv, gk, *, n_heads, hd):
    b, m, d = mem.shape
    xw = n_heads * hd
    spec_m = pl.BlockSpec((1, m, d), lambda i: (i, 0, 0))
    spec_o = pl.BlockSpec((1, m, xw), lambda i: (i, 0, 0))
    return pl.pallas_call(
        functools.partial(_memkv_kernel, n_heads=n_heads, hd=hd),
        grid=(b,),
        in_specs=[spec_m, _const_spec(gain.shape), _const_spec(wkv.shape), _const_spec(gk.shape)],
        out_specs=[spec_o, spec_o],
        out_shape=[jax.ShapeDtypeStruct((b, m, xw), BF16)] * 2,
        compiler_params=_params(1),
        name="memory_kv",
    )(mem, gain, wkv, gk)


def _xattn_router_kernel(x_ref, mg_ref, wmix_ref, gx_ref, wq_ref, gq_ref, k_ref, v_ref, wo_ref,
                         gf_ref, wr_ref, br_ref, x_out, h_out, route_out, cnt_out, carry_ref,
                         *, n_heads, hd, n_groups, per_group, n_sub):
    @pl.when(jnp.logical_and(pl.program_id(0) == 0, pl.program_id(1) == 0))
    def _():
        carry_ref[...] = jnp.zeros_like(carry_ref)

    sub = x_ref.shape[1] // n_sub
    for i in range(n_sub):
        _xattn_router_rows(slice(i * sub, (i + 1) * sub), x_ref, mg_ref, wmix_ref, gx_ref, wq_ref,
                           gq_ref, k_ref, v_ref, wo_ref, gf_ref, wr_ref, br_ref, x_out, h_out,
                           route_out, carry_ref, n_heads=n_heads, hd=hd, n_groups=n_groups,
                           per_group=per_group)
    cnt_out[...] = carry_ref[...]


def _xattn_router_rows(rows, x_ref, mg_ref, wmix_ref, gx_ref, wq_ref, gq_ref, k_ref, v_ref, wo_ref,
                       gf_ref, wr_ref, br_ref, x_out, h_out, route_out, carry_ref,
                       *, n_heads, hd, n_groups, per_group):
    tm = rows.stop - rows.start
    x1 = x_ref[0, rows, :] + _dot(mg_ref[0, rows, :], wmix_ref[...])
    q = _dot(_rms(x1, gx_ref[...]).astype(BF16), wq_ref[...])
    outs = []
    for hh in range(n_heads):
        qn = (_rms(q[:, hh * hd:(hh + 1) * hd], gq_ref[...])).astype(BF16)
        sc = _dot_t(qn, k_ref[0, :, hh * hd:(hh + 1) * hd])
        p = jnp.exp(sc - jnp.max(sc, axis=-1, keepdims=True))
        den = jnp.sum(p, axis=-1, keepdims=True)
        outs.append(_dot(p.astype(BF16), v_ref[0, :, hh * hd:(hh + 1) * hd]) / den)
    x2 = x1 + _dot(jnp.concatenate(outs, axis=-1).astype(BF16), wo_ref[...])
    x_out[0, rows, :] = x2
    hf = _rms(x2, gf_ref[...])
    h_out[0, rows, :] = _pack_halves(hf)
    n_experts = n_groups * per_group
    lg = _dot_t(wr_ref[...], hf.astype(BF16)) + br_ref[:, 0:1]
    sub = lax.broadcasted_iota(I32, (SUBLANES, tm), 0).astype(F32)

    def first_row(mask):
        return jnp.min(jnp.where(mask, sub, float(SUBLANES)), axis=0, keepdims=True)

    gmask = sub < n_groups
    gl = jnp.where(gmask, lg[0:SUBLANES], NEG_INF)
    ge = jnp.where(gmask, jnp.exp(gl - jnp.max(gl, axis=0, keepdims=True)), 0.0)
    gsum = jnp.sum(ge, axis=0, keepdims=True)
    pg_top = 1.0 / gsum
    g_idx = first_row((ge / gsum == pg_top) & gmask)
    el = lg[SUBLANES:2 * SUBLANES]
    for g in range(1, n_groups):
        el = jnp.where(g_idx == g, lg[(g + 1) * SUBLANES:(g + 2) * SUBLANES], el)
    ee = jnp.exp(el - jnp.max(el, axis=0, keepdims=True))
    esum = jnp.sum(ee, axis=0, keepdims=True)
    pe = ee / esum
    p1 = 1.0 / esum
    i1 = first_row(pe == p1)
    pe2 = jnp.where(sub == i1, -1.0, pe)
    p2 = jnp.max(pe2, axis=0, keepdims=True)
    i2 = first_row((pe2 == p2) & (sub != i1))
    den = p1 + p2
    w1 = pg_top * (p1 / den)
    w2 = pg_top * (p2 / den)
    e1 = g_idx * per_group + i1
    e2 = g_idx * per_group + i2
    erow = lax.broadcasted_iota(I32, (n_experts, tm), 0).astype(F32)
    hit1, hit2 = erow == e1, erow == e2
    assign = jnp.where(hit1 | hit2, 1.0, 0.0)
    tj = lax.broadcasted_iota(I32, (tm, tm), 0)
    tt = lax.broadcasted_iota(I32, (tm, tm), 1)
    earlier = jnp.where(tj < tt, 1.0, 0.0).astype(BF16)
    before = _dot(assign.astype(BF16), earlier) + carry_ref[:, 0:1]
    rank1 = jnp.sum(jnp.where(hit1, before, 0.0), axis=0, keepdims=True)
    rank2 = jnp.sum(jnp.where(hit2, before, 0.0), axis=0, keepdims=True)
    carry_ref[...] = carry_ref[...] + jnp.sum(assign, axis=1, keepdims=True)
    packed = jnp.zeros((SUBLANES, tm), F32)
    for idx, val in enumerate([e1, e2, rank1, rank2, w1, w2]):
        packed = jnp.where(sub == idx, val, packed)
    route_out[0, :, rows] = packed


def _xattn_router(x, merged, wmix, gx, wq, gq, kx, vx, wo, gf, wr, br, *, n_heads, hd, n_groups,
                  per_group):
    b, s, d = x.shape
    m, xw = kx.shape[1:]
    tm, n_sub = min(TOKEN_TILE, s), 1
    n_experts = n_groups * per_group
    assert per_group == SUBLANES and n_groups <= SUBLANES
    kv_spec = pl.BlockSpec((1, m, xw), lambda bb, ss: (bb, 0, 0))
    return pl.pallas_call(
        functools.partial(_xattn_router_kernel, n_heads=n_heads, hd=hd, n_groups=n_groups,
                          per_group=per_group, n_sub=n_sub),
        grid=(b, s // tm),
        in_specs=[_tile_spec(tm, d), _tile_spec(tm, d), _const_spec(wmix.shape),
                  _const_spec(gx.shape), _const_spec(wq.shape), _const_spec(gq.shape),
                  kv_spec, kv_spec, _const_spec(wo.shape), _const_spec(gf.shape),
                  _const_spec(wr.shape), _const_spec(br.shape)],
        out_specs=[_tile_spec(tm, d), _tile_spec(tm, d // 2),
                   pl.BlockSpec((1, SUBLANES, tm), lambda bb, ss: (bb, 0, ss)),
                   pl.BlockSpec((n_experts, LANES), lambda bb, ss: (0, 0))],
        out_shape=[jax.ShapeDtypeStruct((b, s, d), F32), jax.ShapeDtypeStruct((b, s, d // 2), U32),
                   jax.ShapeDtypeStruct((b, SUBLANES, s), F32),
                   jax.ShapeDtypeStruct((n_experts, LANES), F32)],
        scratch_shapes=[pltpu.VMEM((n_experts, LANES), F32)],
        compiler_params=_params(2),
        name="mixout_xattn_router",
    )(x, merged, wmix, gx, wq, gq, kx, vx, wo, gf, wr, br)


def _dispatch_kernel(seg_end_ref, cnt_ref, used_ref, pos_ref, h_ref, xs_ref, zero_ref, sems, *,
                     n_experts, tile_rows):
    step = pl.program_id(0)
    tokens = pos_ref.shape[0] // TOP_K
    max_tiles = xs_ref.shape[0] // tile_rows

    def fill_copy(row_start):
        start = pl.multiple_of(row_start, tile_rows)
        return pltpu.make_async_copy(zero_ref, xs_ref.at[pl.ds(start, tile_rows)], sems.at[0])

    @pl.when(step == 0)
    def _():
        zero_ref[...] = jnp.zeros_like(zero_ref)
        for e in range(n_experts):
            @pl.when(cnt_ref[e] > 0)
            def _():
                fill_copy(seg_end_ref[e] - tile_rows).start()

        def start_tail(i, c):
            fill_copy(i * tile_rows).start()
            return c

        def wait_tail(i, c):
            fill_copy(i * tile_rows).wait()
            return c

        lax.fori_loop(used_ref[0], max_tiles, start_tail, 0)
        for e in range(n_experts):
            @pl.when(cnt_ref[e] > 0)
            def _():
                fill_copy(seg_end_ref[e] - tile_rows).wait()
        lax.fori_loop(used_ref[0], max_tiles, wait_tail, 0)

    def row_copy(r, k):
        return pltpu.make_async_copy(h_ref.at[pl.ds(r, 1)],
                                     xs_ref.at[pl.ds(pos_ref[TOP_K * r + k], 1)], sems.at[1 + k])

    for r in range(tokens):
        for k in range(TOP_K):
            row_copy(r, k).start(priority=k)
    for k in range(TOP_K):
        for r in range(tokens):
            row_copy(r, k).wait()


def _dispatch(seg_end, cnt, n_used, pos, h, *, rows_out, tile_rows):
    t, w = h.shape
    tokens = min(ROUTE_TILE, t)
    grid_spec = pltpu.PrefetchScalarGridSpec(
        num_scalar_prefetch=3,
        grid=(t // tokens,),
        in_specs=[pl.BlockSpec((tokens * TOP_K,), lambda i, *_: (i,), memory_space=pltpu.SMEM),
                  pl.BlockSpec((tokens, w), lambda i, *_: (i, 0))],
        out_specs=pl.BlockSpec(memory_space=pl.ANY),
        scratch_shapes=[pltpu.VMEM((tile_rows, w), h.dtype),
                        pltpu.SemaphoreType.DMA((1 + TOP_K,))],
    )
    return pl.pallas_call(
        functools.partial(_dispatch_kernel, n_experts=cnt.shape[0], tile_rows=tile_rows),
        grid_spec=grid_spec,
        out_shape=jax.ShapeDtypeStruct((rows_out, w), h.dtype),
        compiler_params=_params(1),
        name="moe_dispatch",
    )(seg_end, cnt, n_used, pos, h)


def _expert_kernel(exp_ref, next_ref, used_ref, xs_ref, wgu_hbm, wdn_hbm, ys_ref, gu_stage, dn_stage,
                   wgu_bf, wdn_bf, sems, *, layer):
    de = wdn_bf.shape[0]
    i = pl.program_id(0)
    live = i < used_ref[0]
    expert = exp_ref[i]
    new_expert = jnp.logical_or(i == 0, expert != exp_ref[jnp.maximum(i - 1, 0)])

    def fetch(e):
        return (pltpu.make_async_copy(wgu_hbm.at[layer, e], gu_stage, sems.at[0]),
                pltpu.make_async_copy(wdn_hbm.at[layer, e], dn_stage, sems.at[1]))

    @pl.when(jnp.logical_and(live, new_expert))
    def _():
        @pl.when(i == 0)
        def _():
            for cp in fetch(expert):
                cp.start()
        for cp in fetch(expert):
            cp.wait()
        wgu_bf[...] = gu_stage[...].astype(BF16)
        wdn_bf[...] = dn_stage[...].astype(BF16)

        @pl.when(next_ref[i] >= 0)
        def _():
            for cp in fetch(next_ref[i]):
                cp.start()

    @pl.when(live)
    def _():
        hi, lo = _unpack_halves(xs_ref[...])
        xb = jnp.concatenate([hi, lo], axis=1).astype(BF16)
        gu = _dot(xb, wgu_bf[...])
        act = (jax.nn.silu(gu[:, :de]) * gu[:, de:]).astype(BF16)
        ys_ref[...] = _pack_halves(_dot(act, wdn_bf[...]))

    @pl.when(jnp.logical_not(live))
    def _():
        ys_ref[...] = jnp.zeros_like(ys_ref)


def _experts(tile_exp, tile_next, n_used, xs, wgu, wdn, *, layer, tile_rows):
    rows, w = xs.shape
    d = 2 * w
    de = wdn.shape[2]
    grid_spec = pltpu.PrefetchScalarGridSpec(
        num_scalar_prefetch=3,
        grid=(rows // tile_rows,),
        in_specs=[pl.BlockSpec((tile_rows, w), lambda i, *_: (i, 0)),
                  pl.BlockSpec(memory_space=pl.ANY), pl.BlockSpec(memory_space=pl.ANY)],
        out_specs=pl.BlockSpec((tile_rows, w), lambda i, *_: (i, 0)),
        scratch_shapes=[pltpu.VMEM((d, 2 * de), F32), pltpu.VMEM((de, d), F32),
                        pltpu.VMEM((d, 2 * de), BF16), pltpu.VMEM((de, d), BF16),
                        pltpu.SemaphoreType.DMA((2,))],
    )
    return pl.pallas_call(
        functools.partial(_expert_kernel, layer=layer),
        grid_spec=grid_spec,
        out_shape=jax.ShapeDtypeStruct((rows, w), U32),
        compiler_params=_params(1),
        name="moe_experts",
    )(tile_exp, tile_next, n_used, xs, wgu, wdn)


def _combine_kernel(pos_ref, x_ref, route_ref, ys_ref, out_ref, buf, sems):
    tokens = x_ref.shape[0]

    def row_copy(r, k):
        return pltpu.make_async_copy(ys_ref.at[pl.ds(pos_ref[TOP_K * r + k], 1)],
                                     buf.at[k, pl.ds(r, 1)], sems.at[k])

    for r in range(tokens):
        for k in range(TOP_K):
            row_copy(r, k).start(priority=k)
    for k in range(TOP_K):
        for r in range(tokens):
            row_copy(r, k).wait()
    w = x_ref.shape[1] // 2
    acc_hi, acc_lo = x_ref[:, :w], x_ref[:, w:]
    for k in range(TOP_K):
        hi, lo = _unpack_halves(buf[k])
        weight = route_ref[:, k:k + 1]
        acc_hi = acc_hi + weight * hi
        acc_lo = acc_lo + weight * lo
    out_ref[:, :w] = acc_hi
    out_ref[:, w:] = acc_lo


def _combine(pos, x, route, ys):
    t, d = x.shape
    tokens = min(ROUTE_TILE, t)
    return pl.pallas_call(
        _combine_kernel,
        grid=(t // tokens,),
        in_specs=[pl.BlockSpec((tokens * TOP_K,), lambda i: (i,), memory_space=pltpu.SMEM),
                  pl.BlockSpec((tokens, d), lambda i: (i, 0)),
                  pl.BlockSpec((tokens, TOP_K), lambda i: (i, 0)),
                  pl.BlockSpec(memory_space=pl.ANY)],
        out_specs=pl.BlockSpec((tokens, d), lambda i: (i, 0)),
        out_shape=jax.ShapeDtypeStruct((t, d), F32),
        scratch_shapes=[pltpu.VMEM((TOP_K, tokens, d // 2), U32),
                        pltpu.SemaphoreType.DMA((TOP_K,))],
        compiler_params=_params(1),
        name="moe_combine",
    )(pos, x, route, ys)


def _dup_heads(w, n_heads, hd):
    d = w.shape[0]
    w3 = w.reshape(d, n_heads, hd)
    return jnp.concatenate([w3, w3], axis=-1).reshape(d, n_heads * 2 * hd)


def _row(v):
    return v.reshape(1, -1).astype(F32)


def kernel(x, mem, mix_norm, w_in, sc_conv, attn_q_gain, attn_k_gain, attn_sinks, w_out_sc, w_out_attn, cf_conv, cf_conv_b, cf_norm_g, cf_norm_b, w_out_cf, w_mix_out, xa_norm, mem_norm, xa_wq, xa_wkv, xa_q_gain, xa_k_gain, xa_wo, ffn_norm, w_route_group, b_route_group, w_route_expert, b_route_expert, w_expert_gu, w_expert_down):
    b, s, d = x.shape
    depth = w_in.shape[0]
    sc = sc_conv.shape[-1]
    cf = cf_conv.shape[-1]
    hd = attn_q_gain.shape[-1]
    n_q = attn_sinks.shape[-1]
    qw = n_q * hd
    kvw = (w_in.shape[-1] - 3 * sc - qw - 2 * cf - 3 * d) // 2
    n_kv = kvw // hd
    x_hd = xa_q_gain.shape[-1]
    x_heads = xa_wq.shape[-1] // x_hd
    n_groups = w_route_group.shape[-1]
    n_experts = w_route_expert.shape[-1]
    per_group = n_experts // n_groups
    assert 2 * hd == LANES and n_q % (2 * n_kv) == 0 and x_hd == LANES
    assert s % ATTN_BLOCK == 0
    t = b * s
    tile_rows = EXPERT_TILE
    max_tiles = (t * TOP_K) // tile_rows + n_experts

    o_q = 3 * sc
    o_k = o_q + qw
    o_v = o_k + kvw
    o_cf = o_v + kvw
    o_g = o_cf + 2 * cf
    gq = jnp.concatenate([attn_q_gain, attn_q_gain], axis=-1) * (hd ** -0.5)
    gk = jnp.concatenate([attn_k_gain, attn_k_gain], axis=-1)

    for l in range(depth):
        def cols(lo, hi, layer=l):
            return w_in[layer, :, lo:hi].astype(BF16)

        wqkv = jnp.concatenate([cols(o_q, o_k), _dup_heads(cols(o_k, o_v), n_kv, hd),
                                _dup_heads(cols(o_v, o_cf), n_kv, hd)], axis=-1)
        gmix = _row(mix_norm[l])
        m1, hmix = _sc_branch(x, gmix, cols(0, o_q), sc_conv[l], w_out_sc[l].astype(BF16),
                              cols(o_g, o_g + d))
        m2 = _swa_branch(hmix, wqkv, _row(gq[l]), _row(gk[l]), attn_sinks[l], m1,
                         w_out_attn[l].astype(BF16), cols(o_g + d, o_g + 2 * d),
                         n_q=n_q, n_kv=n_kv, hd=hd)
        m3 = _cf_branch(hmix, cols(o_cf, o_g), cf_conv[l], _row(cf_conv_b[l]),
                        _row(cf_norm_g[l]), _row(cf_norm_b[l]), m2, w_out_cf[l].astype(BF16),
                        cols(o_g + 2 * d, o_g + 3 * d))
        kx, vx = _mem_kv(mem, _row(mem_norm[l]), xa_wkv[l].astype(BF16), _row(xa_k_gain[l]),
                         n_heads=x_heads, hd=x_hd)
        pad_g = SUBLANES - n_groups
        pad_e = -(SUBLANES + n_experts) % (2 * SUBLANES)
        wr = jnp.concatenate(
            [w_route_group[l].T, jnp.zeros((pad_g, d), F32), w_route_expert[l].T,
             jnp.zeros((pad_e, d), F32)], axis=0).astype(BF16)
        br = jnp.broadcast_to(jnp.concatenate(
            [b_route_group[l], jnp.zeros((pad_g,), F32), b_route_expert[l],
             jnp.zeros((pad_e,), F32)])[:, None], (wr.shape[0], LANES))
        x2, hf, route, counts = _xattn_router(
            x, m3, w_mix_out[l].astype(BF16), _row(xa_norm[l]), xa_wq[l].astype(BF16),
            _row(xa_q_gain[l] * (x_hd ** -0.5)), kx, vx, xa_wo[l].astype(BF16),
            _row(ffn_norm[l]), wr, br, n_heads=x_heads, hd=x_hd, n_groups=n_groups,
            per_group=per_group)
        cnt = counts[:, 0].astype(I32)
        n_tile = (cnt + tile_rows - 1) // tile_rows
        tile_end = jnp.cumsum(n_tile)
        seg_start = (tile_end - n_tile) * tile_rows
        route2 = jnp.swapaxes(route, 1, 2).reshape(t, SUBLANES)
        eid = route2[:, 0:TOP_K].astype(I32)
        rank = route2[:, TOP_K:2 * TOP_K].astype(I32)
        gate_w = route2[:, 2 * TOP_K:3 * TOP_K]
        experts = jnp.arange(n_experts, dtype=I32)
        pos = (jnp.sum(jnp.where(eid[..., None] == experts, seg_start, 0), axis=-1)
               + rank).reshape(t * TOP_K)
        n_used = tile_end[-1:].astype(I32)
        tile_live = jnp.minimum(jnp.arange(max_tiles, dtype=I32), n_used - 1)
        tile_exp = jnp.minimum(jnp.sum(tile_end[None, :] <= tile_live[:, None], axis=1),
                               n_experts - 1).astype(I32)
        xs = _dispatch(tile_end * tile_rows, cnt, n_used, pos, hf.reshape(t, d // 2),
                       rows_out=max_tiles * tile_rows, tile_rows=tile_rows)
        later = jnp.where((experts[None, :] > experts[:, None]) & (n_tile[None, :] > 0),
                          experts[None, :], n_experts)
        next_exp = jnp.min(later, axis=1)
        next_exp = jnp.where(next_exp < n_experts, next_exp, -1).astype(I32)
        tile_next = jnp.sum(jnp.where(tile_exp[:, None] == experts, next_exp, 0), axis=1)
        ys = _experts(tile_exp, tile_next.astype(I32), n_used, xs, w_expert_gu, w_expert_down,
                      layer=l, tile_rows=tile_rows)
        x = _combine(pos, x2.reshape(t, d), gate_w, ys).reshape(b, s, d)
    return x
```

```python
import functools

import jax
import jax.numpy as jnp
from jax import lax
from jax.experimental import pallas as pl
from jax.experimental.pallas import tpu as pltpu

F32 = jnp.float32
BF16 = jnp.bfloat16
I32 = jnp.int32
U32 = jnp.uint32

EPS = 1e-6
NEG_INF = -1e30
ATTN_BLOCK = 128
TOP_K = 2
LANES = 128
SUBLANES = 8
TOKEN_TILE = 256
EXPERT_TILE = 256
ROUTE_TILE = 1024
VMEM_LIMIT = 56 * 1024 * 1024


def _const_spec(shape):
    nd = len(shape)
    return pl.BlockSpec(shape, lambda *_: (0,) * nd, pipeline_mode=pl.Buffered(1))


def _tile_spec(tm, width):
    return pl.BlockSpec((1, tm, width), lambda b, s: (b, s, 0))


def _params(n_axes):
    return pltpu.CompilerParams(dimension_semantics=("arbitrary",) * n_axes,
                                vmem_limit_bytes=VMEM_LIMIT)


def _rms(xf, gain):
    return xf * lax.rsqrt(jnp.mean(xf * xf, axis=-1, keepdims=True) + EPS) * gain


def _dot(a, b):
    return jnp.dot(a, b, preferred_element_type=F32)


def _dot_t(a, b):
    return lax.dot_general(a, b, (((1,), (1,)), ((), ())), preferred_element_type=F32)


def _pack_halves(v):
    w = v.shape[1] // 2
    hi = pltpu.bitcast(v[:, :w].astype(BF16).astype(F32), U32)
    lo = pltpu.bitcast(v[:, w:].astype(BF16).astype(F32), U32)
    return hi | (lo >> 16)


def _unpack_halves(u):
    return pltpu.bitcast(u & jnp.uint32(0xFFFF0000), F32), pltpu.bitcast(u << 16, F32)


def _zero_after(v, shape):
    bits = pltpu.bitcast(v[0:SUBLANES, 0:LANES], U32)
    zero = pltpu.bitcast((bits >> 16) >> 16, F32)
    return jnp.tile(zero, (shape[0] // SUBLANES, shape[1] // LANES))


def _halo_rows(taps):
    return -(-(taps - 1) // SUBLANES) * SUBLANES


def _conv_cols(ext_ref, w_ref, out_ref, tm, taps, cols, row_base=0):
    halo = _halo_rows(taps)
    rows = min(64, tm)
    groups = {}
    for k in range(taps):
        off = halo - (taps - 1) + k
        groups.setdefault(off % SUBLANES, []).append((off // SUBLANES, k))
    for r0 in range(row_base, row_base + tm, rows):
        total = None
        for mis in sorted(groups):
            members = groups[mis]
            span = SUBLANES * max(a for a, _ in members) + rows
            shifted = ext_ref[pl.ds(r0 + mis, span), cols]
            part = None
            for a, k in members:
                term = w_ref[pl.ds(k, 1), cols] * shifted[SUBLANES * a:SUBLANES * a + rows]
                part = term if part is None else part + term
            total = part if total is None else total + part
        out_ref[pl.ds(r0, rows), cols] = total


def _conv_carry(ext_ref, tm, taps):
    halo = _halo_rows(taps)
    ext_ref[0:halo, :] = ext_ref[tm:tm + halo, :]


def _sc_kernel(x_ref, g_ref, wsc_ref, cw_ref, wo_ref, wg_ref, out_ref, h_out, ext_ref, cv_ref, *,
               taps):
    tm = x_ref.shape[1]
    sc = cw_ref.shape[1]
    halo = _halo_rows(taps)

    @pl.when(pl.program_id(1) == 0)
    def _():
        ext_ref[0:halo, :] = jnp.zeros((halo, sc), F32)

    h = _rms(x_ref[0], g_ref[...]).astype(BF16)
    h_out[0] = h
    p = _dot(h, wsc_ref[...])
    ext_ref[halo:halo + tm, :] = p[:, sc:2 * sc] * p[:, 2 * sc:]

    def col_chunk(c, carry):
        _conv_cols(ext_ref, cw_ref, cv_ref, tm, taps, pl.ds(pl.multiple_of(c * LANES, LANES), LANES))
        return carry

    lax.fori_loop(0, sc // LANES, col_chunk, 0)
    _conv_carry(ext_ref, tm, taps)
    v = (p[:, :sc] * cv_ref[...]).astype(BF16)
    y = _dot(v, wo_ref[...])
    gate = jax.nn.sigmoid(_dot(h, wg_ref[...]))
    out_ref[0] = (gate * y).astype(BF16)


def _sc_branch(x, gain, wsc, cw, wo, wg):
    b, s, d = x.shape
    taps, sc = cw.shape
    tm = min(TOKEN_TILE, s)
    halo = _halo_rows(taps)
    return pl.pallas_call(
        functools.partial(_sc_kernel, taps=taps),
        grid=(b, s // tm),
        in_specs=[_tile_spec(tm, d), _const_spec(gain.shape), _const_spec(wsc.shape),
                  _const_spec(cw.shape), _const_spec(wo.shape), _const_spec(wg.shape)],
        out_specs=[_tile_spec(tm, d), _tile_spec(tm, d)],
        out_shape=[jax.ShapeDtypeStruct((b, s, d), BF16)] * 2,
        scratch_shapes=[pltpu.VMEM((halo + tm, sc), F32), pltpu.VMEM((tm, sc), F32)],
        compiler_params=_params(2),
        name="mixer_gated_conv",
    )(x, gain, wsc, cw, wo, wg)


def _swa_kernel(sink_ref, h_ref, wqkv_ref, gq_ref, gk_ref, prev_ref, wo_ref, wg_ref,
                out_ref, kbuf, vbuf, attn_ref, *, n_q, n_kv, hd):
    tm = h_ref.shape[1]
    blk = ATTN_BLOCK
    grp = n_q // n_kv
    qw = n_q * hd
    kw = n_kv * LANES
    s_idx = pl.program_id(1)

    @pl.when(s_idx == 0)
    def _():
        kbuf[0:blk, :] = jnp.zeros((blk, kw), BF16)
        vbuf[0:blk, :] = jnp.zeros((blk, kw), BF16)

    h = h_ref[0]
    qkv = _dot(h, wqkv_ref[...])
    for g in range(n_kv):
        ks = qkv[:, qw + g * LANES:qw + (g + 1) * LANES]
        ms = jnp.sum(ks * ks, axis=-1, keepdims=True) * (1.0 / LANES)
        kbuf[blk:blk + tm, g * LANES:(g + 1) * LANES] = (
            ks * lax.rsqrt(ms + EPS) * gk_ref[...]).astype(BF16)
    vbuf[blk:blk + tm, :] = qkv[:, qw + kw:].astype(BF16)

    lane = lax.broadcasted_iota(I32, (blk, LANES), 1)
    qi = lax.broadcasted_iota(I32, (blk, 2 * blk), 0)
    kj = lax.broadcasted_iota(I32, (blk, 2 * blk), 1)
    for jb in range(tm // blk):
        r0 = jb * blk
        first = jnp.logical_and(s_idx == 0, jb == 0)
        lo = jnp.where(first, blk, 0)
        valid = (kj >= jnp.maximum(qi + 1, lo)) & (kj <= qi + blk)
        for g in range(n_kv):
            kk = kbuf[r0:r0 + 2 * blk, g * LANES:(g + 1) * LANES]
            vv = vbuf[r0:r0 + 2 * blk, g * LANES:(g + 1) * LANES]
            qn = []
            for r in range(grp):
                head = g * grp + r
                slab, half = head // 2, head % 2
                qs = qkv[r0:r0 + blk, slab * LANES:(slab + 1) * LANES]
                qm = jnp.where(lane >= hd if half else lane < hd, qs, 0.0)
                ms = jnp.sum(qm * qm, axis=-1, keepdims=True) * (1.0 / hd)
                qn.append((qm * lax.rsqrt(ms + EPS) * gq_ref[...]).astype(BF16))
            scores = _dot_t(jnp.concatenate(qn, axis=0), kk)
            probs, dens = [], []
            for r in range(grp):
                sink = sink_ref[g * grp + r]
                sc = jnp.where(valid, scores[r * blk:(r + 1) * blk], NEG_INF)
                m = jnp.maximum(jnp.max(sc, axis=-1, keepdims=True), sink)
                p = jnp.exp(sc - m)
                dens.append(jnp.sum(p, axis=-1, keepdims=True) + jnp.exp(sink - m))
                probs.append(p.astype(BF16))
            o = _dot(jnp.concatenate(probs, axis=0), vv)
            outs = [o[r * blk:(r + 1) * blk] / dens[r] for r in range(grp)]
            for pr in range(grp // 2):
                slab = (g * grp) // 2 + pr
                attn_ref[r0:r0 + blk, slab * LANES:(slab + 1) * LANES] = jnp.where(
                    lane < hd, outs[2 * pr], outs[2 * pr + 1])

    kbuf[0:blk, :] = kbuf[tm:tm + blk, :]
    vbuf[0:blk, :] = vbuf[tm:tm + blk, :]
    y = _dot(attn_ref[...].astype(BF16), wo_ref[...])
    gate = jax.nn.sigmoid(_dot(h, wg_ref[...]))
    out_ref[0] = (prev_ref[0].astype(F32) + gate * y).astype(BF16)


def _swa_branch(h, wqkv, gq, gk, sinks, prev, wo, wg, *, n_q, n_kv, hd):
    b, s, d = h.shape
    tm = min(TOKEN_TILE, s)
    kw = n_kv * LANES
    tile = lambda bb, ss, *_: (bb, ss, 0)
    const = lambda shape: pl.BlockSpec(shape, lambda *_: (0,) * len(shape),
                                       pipeline_mode=pl.Buffered(1))
    grid_spec = pltpu.PrefetchScalarGridSpec(
        num_scalar_prefetch=1,
        grid=(b, s // tm),
        in_specs=[pl.BlockSpec((1, tm, d), tile), const(wqkv.shape),
                  const(gq.shape), const(gk.shape), pl.BlockSpec((1, tm, d), tile),
                  const(wo.shape), const(wg.shape)],
        out_specs=pl.BlockSpec((1, tm, d), tile),
        scratch_shapes=[pltpu.VMEM((ATTN_BLOCK + tm, kw), BF16),
                        pltpu.VMEM((ATTN_BLOCK + tm, kw), BF16),
                        pltpu.VMEM((tm, n_q * hd), F32)],
    )
    return pl.pallas_call(
        functools.partial(_swa_kernel, n_q=n_q, n_kv=n_kv, hd=hd),
        grid_spec=grid_spec,
        out_shape=jax.ShapeDtypeStruct((b, s, d), BF16),
        compiler_params=_params(2),
        name="mixer_swa",
    )(sinks, h, wqkv, gq, gk, prev, wo, wg)


def _cf_kernel(hin_ref, wcf_ref, cw_ref, cb_ref, lng_ref, lnb_ref, prev_ref, wo_ref, wg_ref,
               out_ref, ext_ref, cv_ref, gate_ref, *, taps):
    tm, d = hin_ref.shape[1:]
    cf = cw_ref.shape[1]
    halo = _halo_rows(taps)
    h_ref = hin_ref.at[0]

    @pl.when(pl.program_id(1) == 0)
    def _():
        ext_ref[0:halo, :] = jnp.zeros((halo, cf), F32)

    width = 2 * LANES
    gwidth = d * width // cf
    gate = None
    for ci, c0 in enumerate(range(0, cf, width)):
        val = _dot(h_ref[...], wcf_ref[:, c0:c0 + width])
        gl = _dot(h_ref[...], wcf_ref[:, cf + c0:cf + c0 + width])
        u_chunk = val * jax.nn.sigmoid(gl)
        if gate is not None:
            u_chunk = u_chunk + _zero_after(gate, (tm, width))
        ext_ref[halo:halo + tm, c0:c0 + width] = u_chunk
        gate = jax.nn.sigmoid(_dot(h_ref[...], wg_ref[:, ci * gwidth:(ci + 1) * gwidth]))
        gate_ref[:, ci * gwidth:(ci + 1) * gwidth] = gate
        for c in range(c0, c0 + width, LANES):
            _conv_cols(ext_ref, cw_ref, cv_ref, tm, taps, slice(c, c + LANES))
    _conv_carry(ext_ref, tm, taps)
    u = cv_ref[...] + cb_ref[...] + _zero_after(gate, (tm, cf))
    xc = u - jnp.mean(u, axis=-1, keepdims=True)
    ln = xc * lax.rsqrt(jnp.mean(xc * xc, axis=-1, keepdims=True) + EPS) * lng_ref[...] + lnb_ref[...]
    act = (ln * jax.nn.sigmoid(ln)).astype(BF16)
    y = _dot(act, wo_ref[...])
    out_ref[0] = (prev_ref[0].astype(F32) + gate_ref[...] * y).astype(BF16)


def _cf_branch(h, wcf, cw, cb, lng, lnb, prev, wo, wg):
    b, s, d = h.shape
    taps, cf = cw.shape
    tm = min(TOKEN_TILE, s)
    halo = _halo_rows(taps)
    return pl.pallas_call(
        functools.partial(_cf_kernel, taps=taps),
        grid=(b, s // tm),
        in_specs=[_tile_spec(tm, d), _const_spec(wcf.shape),
                  _const_spec(cw.shape), _const_spec(cb.shape), _const_spec(lng.shape),
                  _const_spec(lnb.shape), _tile_spec(tm, d), _const_spec(wo.shape),
                  _const_spec(wg.shape)],
        out_specs=_tile_spec(tm, d),
        out_shape=jax.ShapeDtypeStruct((b, s, d), BF16),
        scratch_shapes=[pltpu.VMEM((halo + tm, cf), F32), pltpu.VMEM((tm, cf), F32),
                        pltpu.VMEM((tm, d), F32)],
        compiler_params=_params(2),
        name="mixer_conformer_conv",
    )(h, wcf, cw, cb, lng, lnb, prev, wo, wg)


def _memkv_kernel(mem_ref, g_ref, wkv_ref, gk_ref, k_ref, v_ref, *, n_heads, hd):
    xw = n_heads * hd
    mn = _rms(mem_ref[0], g_ref[...]).astype(BF16)
    kv = _dot(mn, wkv_ref[...])
    for hh in range(n_heads):
        k_ref[0, :, hh * hd:(hh + 1) * hd] = _rms(kv[:, hh * hd:(hh + 1) * hd], gk_ref[...]).astype(BF16)
    v_ref[0] = kv[:, xw:].astype(BF16)


def _mem_kv(mem, gain, wkv, gk, *, n_heads, hd):
    b, m, d = mem.shape
    xw = n_heads * hd
    spec_m = pl.BlockSpec((1, m, d), lambda i: (i, 0, 0))
    spec_o = pl.BlockSpec((1, m, xw), lambda i: (i, 0, 0))
    return pl.pallas_call(
        functools.partial(_memkv_kernel, n_heads=n_heads, hd=hd),
        grid=(b,),
        in_specs=[spec_m, _const_spec(gain.shape), _const_spec(wkv.shape), _const_spec(gk.shape)],
        out_specs=[spec_o, spec_o],
        out_shape=[jax.ShapeDtypeStruct((b, m, xw), BF16)] * 2,
        compiler_params=_params(1),
        name="memory_kv",
    )(mem, gain, wkv, gk)


def _xattn_router_kernel(x_ref, mg_ref, wmix_ref, gx_ref, wq_ref, gq_ref, k_ref, v_ref, wo_ref,
                         gf_ref, wr_ref, br_ref, x_out, h_out, route_out, cnt_out, carry_ref,
                         *, n_heads, hd, n_groups, per_group, n_sub):
    @pl.when(jnp.logical_and(pl.program_id(0) == 0, pl.program_id(1) == 0))
    def _():
        carry_ref[...] = jnp.zeros_like(carry_ref)

    sub = x_ref.shape[1] // n_sub
    for i in range(n_sub):
        _xattn_router_rows(slice(i * sub, (i + 1) * sub), x_ref, mg_ref, wmix_ref, gx_ref, wq_ref,
                           gq_ref, k_ref, v_ref, wo_ref, gf_ref, wr_ref, br_ref, x_out, h_out,
                           route_out, carry_ref, n_heads=n_heads, hd=hd, n_groups=n_groups,
                           per_group=per_group)
    cnt_out[...] = carry_ref[...]


def _xattn_router_rows(rows, x_ref, mg_ref, wmix_ref, gx_ref, wq_ref, gq_ref, k_ref, v_ref, wo_ref,
                       gf_ref, wr_ref, br_ref, x_out, h_out, route_out, carry_ref,
                       *, n_heads, hd, n_groups, per_group):
    tm = rows.stop - rows.start
    x1 = x_ref[0, rows, :] + _dot(mg_ref[0, rows, :], wmix_ref[...])
    q = _dot(_rms(x1, gx_ref[...]).astype(BF16), wq_ref[...])
    outs = []
    for hh in range(n_heads):
        qn = (_rms(q[:, hh * hd:(hh + 1) * hd], gq_ref[...])).astype(BF16)
        sc = _dot_t(qn, k_ref[0, :, hh * hd:(hh + 1) * hd])
        p = jnp.exp(sc - jnp.max(sc, axis=-1, keepdims=True))
        den = jnp.sum(p, axis=-1, keepdims=True)
        outs.append(_dot(p.astype(BF16), v_ref[0, :, hh * hd:(hh + 1) * hd]) / den)
    x2 = x1 + _dot(jnp.concatenate(outs, axis=-1).astype(BF16), wo_ref[...])
    x_out[0, rows, :] = x2
    hf = _rms(x2, gf_ref[...])
    h_out[0, rows, :] = _pack_halves(hf)
    n_experts = n_groups * per_group
    lg = _dot_t(wr_ref[...], hf.astype(BF16)) + br_ref[:, 0:1]
    sub = lax.broadcasted_iota(I32, (SUBLANES, tm), 0).astype(F32)

    def first_row(mask):
        return jnp.min(jnp.where(mask, sub, float(SUBLANES)), axis=0, keepdims=True)

    gmask = sub < n_groups
    gl = jnp.where(gmask, lg[0:SUBLANES], NEG_INF)
    ge = jnp.where(gmask, jnp.exp(gl - jnp.max(gl, axis=0, keepdims=True)), 0.0)
    gsum = jnp.sum(ge, axis=0, keepdims=True)
    pg_top = 1.0 / gsum
    g_idx = first_row((ge / gsum == pg_top) & gmask)
    el = lg[SUBLANES:2 * SUBLANES]
    for g in range(1, n_groups):
        el = jnp.where(g_idx == g, lg[(g + 1) * SUBLANES:(g + 2) * SUBLANES], el)
    ee = jnp.exp(el - jnp.max(el, axis=0, keepdims=True))
    esum = jnp.sum(ee, axis=0, keepdims=True)
    pe = ee / esum
    p1 = 1.0 / esum
    i1 = first_row(pe == p1)
    pe2 = jnp.where(sub == i1, -1.0, pe)
    p2 = jnp.max(pe2, axis=0, keepdims=True)
    i2 = first_row((pe2 == p2) & (sub != i1))
    den = p1 + p2
    w1 = pg_top * (p1 / den)
    w2 = pg_top * (p2 / den)
    e1 = g_idx * per_group + i1
    e2 = g_idx * per_group + i2
    erow = lax.broadcasted_iota(I32, (n_experts, tm), 0).astype(F32)
    hit1, hit2 = erow == e1, erow == e2
    assign = jnp.where(hit1 | hit2, 1.0, 0.0)
    tj = lax.broadcasted_iota(I32, (tm, tm), 0)
    tt = lax.broadcasted_iota(I32, (tm, tm), 1)
    earlier = jnp.where(tj < tt, 1.0, 0.0).astype(BF16)
    before = _dot(assign.astype(BF16), earlier) + carry_ref[:, 0:1]
    rank1 = jnp.sum(jnp.where(hit1, before, 0.0), axis=0, keepdims=True)
    rank2 = jnp.sum(jnp.where(hit2, before, 0.0), axis=0, keepdims=True)
    carry_ref[...] = carry_ref[...] + jnp.sum(assign, axis=1, keepdims=True)
    packed = jnp.zeros((SUBLANES, tm), F32)
    for idx, val in enumerate([e1, e2, rank1, rank2, w1, w2]):
        packed = jnp.where(sub == idx, val, packed)
    route_out[0, :, rows] = packed


def _xattn_router(x, merged, wmix, gx, wq, gq, kx, vx, wo, gf, wr, br, *, n_heads, hd, n_groups,
                  per_group):
    b, s, d = x.shape
    m, xw = kx.shape[1:]
    tm, n_sub = min(TOKEN_TILE, s), 1
    n_experts = n_groups * per_group
    assert per_group == SUBLANES and n_groups <= SUBLANES
    kv_spec = pl.BlockSpec((1, m, xw), lambda bb, ss: (bb, 0, 0))
    return pl.pallas_call(
        functools.partial(_xattn_router_kernel, n_heads=n_heads, hd=hd, n_groups=n_groups,
                          per_group=per_group, n_sub=n_sub),
        grid=(b, s // tm),
        in_specs=[_tile_spec(tm, d), _tile_spec(tm, d), _const_spec(wmix.shape),
                  _const_spec(gx.shape), _const_spec(wq.shape), _const_spec(gq.shape),
                  kv_spec, kv_spec, _const_spec(wo.shape), _const_spec(gf.shape),
                  _const_spec(wr.shape), _const_spec(br.shape)],
        out_specs=[_tile_spec(tm, d), _tile_spec(tm, d // 2),
                   pl.BlockSpec((1, SUBLANES, tm), lambda bb, ss: (bb, 0, ss)),
                   pl.BlockSpec((n_experts, LANES), lambda bb, ss: (0, 0))],
        out_shape=[jax.ShapeDtypeStruct((b, s, d), F32), jax.ShapeDtypeStruct((b, s, d // 2), U32),
                   jax.ShapeDtypeStruct((b, SUBLANES, s), F32),
                   jax.ShapeDtypeStruct((n_experts, LANES), F32)],
        scratch_shapes=[pltpu.VMEM((n_experts, LANES), F32)],
        compiler_params=_params(2),
        name="mixout_xattn_router",
    )(x, merged, wmix, gx, wq, gq, kx, vx, wo, gf, wr, br)


def _dispatch_kernel(seg_end_ref, cnt_ref, used_ref, pos_ref, h_ref, xs_ref, zero_ref, sems, *,
                     n_experts, tile_rows):
    step = pl.program_id(0)
    tokens = pos_ref.shape[0] // TOP_K
    max_tiles = xs_ref.shape[0] // tile_rows

    def fill_copy(row_start):
        start = pl.multiple_of(row_start, tile_rows)
        return pltpu.make_async_copy(zero_ref, xs_ref.at[pl.ds(start, tile_rows)], sems.at[0])

    @pl.when(step == 0)
    def _():
        zero_ref[...] = jnp.zeros_like(zero_ref)
        for e in range(n_experts):
            @pl.when(cnt_ref[e] > 0)
            def _():
                fill_copy(seg_end_ref[e] - tile_rows).start()

        def start_tail(i, c):
            fill_copy(i * tile_rows).start()
            return c

        def wait_tail(i, c):
            fill_copy(i * tile_rows).wait()
            return c

        lax.fori_loop(used_ref[0], max_tiles, start_tail, 0)
        for e in range(n_experts):
            @pl.when(cnt_ref[e] > 0)
            def _():
                fill_copy(seg_end_ref[e] - tile_rows).wait()
        lax.fori_loop(used_ref[0], max_tiles, wait_tail, 0)

    def row_copy(r, k):
        return pltpu.make_async_copy(h_ref.at[pl.ds(r, 1)],
                                     xs_ref.at[pl.ds(pos_ref[TOP_K * r + k], 1)], sems.at[1 + k])

    for r in range(tokens):
        for k in range(TOP_K):
            row_copy(r, k).start(priority=k)
    for k in range(TOP_K):
        for r in range(tokens):
            row_copy(r, k).wait()


def _dispatch(seg_end, cnt, n_used, pos, h, *, rows_out, tile_rows):
    t, w = h.shape
    tokens = min(ROUTE_TILE, t)
    grid_spec = pltpu.PrefetchScalarGridSpec(
        num_scalar_prefetch=3,
        grid=(t // tokens,),
        in_specs=[pl.BlockSpec((tokens * TOP_K,), lambda i, *_: (i,), memory_space=pltpu.SMEM),
                  pl.BlockSpec((tokens, w), lambda i, *_: (i, 0))],
        out_specs=pl.BlockSpec(memory_space=pl.ANY),
        scratch_shapes=[pltpu.VMEM((tile_rows, w), h.dtype),
                        pltpu.SemaphoreType.DMA((1 + TOP_K,))],
    )
    return pl.pallas_call(
        functools.partial(_dispatch_kernel, n_experts=cnt.shape[0], tile_rows=tile_rows),
        grid_spec=grid_spec,
        out_shape=jax.ShapeDtypeStruct((rows_out, w), h.dtype),
        compiler_params=_params(1),
        name="moe_dispatch",
    )(seg_end, cnt, n_used, pos, h)


def _expert_kernel(exp_ref, next_ref, used_ref, xs_ref, wgu_hbm, wdn_hbm, ys_ref, gu_stage, dn_stage,
                   wgu_bf, wdn_bf, sems, *, layer):
    de = wdn_bf.shape[0]
    i = pl.program_id(0)
    live = i < used_ref[0]
    expert = exp_ref[i]
    new_expert = jnp.logical_or(i == 0, expert != exp_ref[jnp.maximum(i - 1, 0)])

    def fetch(e):
        return (pltpu.make_async_copy(wgu_hbm.at[layer, e], gu_stage, sems.at[0]),
                pltpu.make_async_copy(wdn_hbm.at[layer, e], dn_stage, sems.at[1]))

    @pl.when(jnp.logical_and(live, new_expert))
    def _():
        @pl.when(i == 0)
        def _():
            for cp in fetch(expert):
                cp.start()
        for cp in fetch(expert):
            cp.wait()
        wgu_bf[...] = gu_stage[...].astype(BF16)
        wdn_bf[...] = dn_stage[...].astype(BF16)

        @pl.when(next_ref[i] >= 0)
        def _():
            for cp in fetch(next_ref[i]):
                cp.start()

    @pl.when(live)
    def _():
        hi, lo = _unpack_halves(xs_ref[...])
        xb = jnp.concatenate([hi, lo], axis=1).astype(BF16)
        gu = _dot(xb, wgu_bf[...])
        act = (jax.nn.silu(gu[:, :de]) * gu[:, de:]).astype(BF16)
        ys_ref[...] = _pack_halves(_dot(act, wdn_bf[...]))

    @pl.when(jnp.logical_not(live))
    def _():
        ys_ref[...] = jnp.zeros_like(ys_ref)


def _experts(tile_exp, tile_next, n_used, xs, wgu, wdn, *, layer, tile_rows):
    rows, w = xs.shape
    d = 2 * w
    de = wdn.shape[2]
    grid_spec = pltpu.PrefetchScalarGridSpec(
        num_scalar_prefetch=3,
        grid=(rows // tile_rows,),
        in_specs=[pl.BlockSpec((tile_rows, w), lambda i, *_: (i, 0)),
                  pl.BlockSpec(memory_space=pl.ANY), pl.BlockSpec(memory_space=pl.ANY)],
        out_specs=pl.BlockSpec((tile_rows, w), lambda i, *_: (i, 0)),
        scratch_shapes=[pltpu.VMEM((d, 2 * de), F32), pltpu.VMEM((de, d), F32),
                        pltpu.VMEM((d, 2 * de), BF16), pltpu.VMEM((de, d), BF16),
                        pltpu.SemaphoreType.DMA((2,))],
    )
    return pl.pallas_call(
        functools.partial(_expert_kernel, layer=layer),
        grid_spec=grid_spec,
        out_shape=jax.ShapeDtypeStruct((rows, w), U32),
        compiler_params=_params(1),
        name="moe_experts",
    )(tile_exp, tile_next, n_used, xs, wgu, wdn)


def _combine_kernel(pos_ref, x_ref, route_ref, ys_ref, out_ref, buf, sems):
    tokens = x_ref.shape[0]

    def row_copy(r, k):
        return pltpu.make_async_copy(ys_ref.at[pl.ds(pos_ref[TOP_K * r + k], 1)],
                                     buf.at[k, pl.ds(r, 1)], sems.at[k])

    for r in range(tokens):
        for k in range(TOP_K):
            row_copy(r, k).start(priority=k)
    for k in range(TOP_K):
        for r in range(tokens):
            row_copy(r, k).wait()
    w = x_ref.shape[1] // 2
    acc_hi, acc_lo = x_ref[:, :w], x_ref[:, w:]
    for k in range(TOP_K):
        hi, lo = _unpack_halves(buf[k])
        weight = route_ref[:, k:k + 1]
        acc_hi = acc_hi + weight * hi
        acc_lo = acc_lo + weight * lo
    out_ref[:, :w] = acc_hi
    out_ref[:, w:] = acc_lo


def _combine(pos, x, route, ys):
    t, d = x.shape
    tokens = min(ROUTE_TILE, t)
    return pl.pallas_call(
        _combine_kernel,
        grid=(t // tokens,),
        in_specs=[pl.BlockSpec((tokens * TOP_K,), lambda i: (i,), memory_space=pltpu.SMEM),
                  pl.BlockSpec((tokens, d), lambda i: (i, 0)),
                  pl.BlockSpec((tokens, TOP_K), lambda i: (i, 0)),
                  pl.BlockSpec(memory_space=pl.ANY)],
        out_specs=pl.BlockSpec((tokens, d), lambda i: (i, 0)),
        out_shape=jax.ShapeDtypeStruct((t, d), F32),
        scratch_shapes=[pltpu.VMEM((TOP_K, tokens, d // 2), U32),
                        pltpu.SemaphoreType.DMA((TOP_K,))],
        compiler_params=_params(1),
        name="moe_combine",
    )(pos, x, route, ys)


def _dup_heads(w, n_heads, hd):
    d = w.shape[0]
    w3 = w.reshape(d, n_heads, hd)
    return jnp.concatenate([w3, w3], axis=-1).reshape(d, n_heads * 2 * hd)


def _row(v):
    return v.reshape(1, -1).astype(F32)


def kernel(x, mem, mix_norm, w_in, sc_conv, attn_q_gain, attn_k_gain, attn_sinks, w_out_sc, w_out_attn, cf_conv, cf_conv_b, cf_norm_g, cf_norm_b, w_out_cf, w_mix_out, xa_norm, mem_norm, xa_wq, xa_wkv, xa_q_gain, xa_k_gain, xa_wo, ffn_norm, w_route_group, b_route_group, w_route_expert, b_route_expert, w_expert_gu, w_expert_down):
    b, s, d = x.shape
    depth = w_in.shape[0]
    sc = sc_conv.shape[-1]
    cf = cf_conv.shape[-1]
    hd = attn_q_gain.shape[-1]
    n_q = attn_sinks.shape[-1]
    qw = n_q * hd
    kvw = (w_in.shape[-1] - 3 * sc - qw - 2 * cf - 3 * d) // 2
    n_kv = kvw // hd
    x_hd = xa_q_gain.shape[-1]
    x_heads = xa_wq.shape[-1] // x_hd
    n_groups = w_route_group.shape[-1]
    n_experts = w_route_expert.shape[-1]
    per_group = n_experts // n_groups
    assert 2 * hd == LANES and n_q % (2 * n_kv) == 0 and x_hd == LANES
    assert s % ATTN_BLOCK == 0
    t = b * s
    tile_rows = EXPERT_TILE
    max_tiles = (t * TOP_K) // tile_rows + n_experts

    o_q = 3 * sc
    o_k = o_q + qw
    o_v = o_k + kvw
    o_cf = o_v + kvw
    o_g = o_cf + 2 * cf
    gq = jnp.concatenate([attn_q_gain, attn_q_gain], axis=-1) * (hd ** -0.5)
    gk = jnp.concatenate([attn_k_gain, attn_k_gain], axis=-1)

    for l in range(depth):
        def cols(lo, hi, layer=l):
            return w_in[layer, :, lo:hi].astype(BF16)

        wqkv = jnp.concatenate([cols(o_q, o_k), _dup_heads(cols(o_k, o_v), n_kv, hd),
                                _dup_heads(cols(o_v, o_cf), n_kv, hd)], axis=-1)
        gmix = _row(mix_norm[l])
        m1, hmix = _sc_branch(x, gmix, cols(0, o_q), sc_conv[l], w_out_sc[l].astype(BF16),
                              cols(o_g, o_g + d))
        m2 = _swa_branch(hmix, wqkv, _row(gq[l]), _row(gk[l]), attn_sinks[l], m1,
                         w_out_attn[l].astype(BF16), cols(o_g + d, o_g + 2 * d),
                         n_q=n_q, n_kv=n_kv, hd=hd)
        m3 = _cf_branch(hmix, cols(o_cf, o_g), cf_conv[l], _row(cf_conv_b[l]),
                        _row(cf_norm_g[l]), _row(cf_norm_b[l]), m2, w_out_cf[l].astype(BF16),
                        cols(o_g + 2 * d, o_g + 3 * d))
        kx, vx = _mem_kv(mem, _row(mem_norm[l]), xa_wkv[l].astype(BF16), _row(xa_k_gain[l]),
                         n_heads=x_heads, hd=x_hd)
        pad_g = SUBLANES - n_groups
        pad_e = -(SUBLANES + n_experts) % (2 * SUBLANES)
        wr = jnp.concatenate(
            [w_route_group[l].T, jnp.zeros((pad_g, d), F32), w_route_expert[l].T,
             jnp.zeros((pad_e, d), F32)], axis=0).astype(BF16)
        br = jnp.broadcast_to(jnp.concatenate(
            [b_route_group[l], jnp.zeros((pad_g,), F32), b_route_expert[l],
             jnp.zeros((pad_e,), F32)])[:, None], (wr.shape[0], LANES))
        x2, hf, route, counts = _xattn_router(
            x, m3, w_mix_out[l].astype(BF16), _row(xa_norm[l]), xa_wq[l].astype(BF16),
            _row(xa_q_gain[l] * (x_hd ** -0.5)), kx, vx, xa_wo[l].astype(BF16),
            _row(ffn_norm[l]), wr, br, n_heads=x_heads, hd=x_hd, n_groups=n_groups,
            per_group=per_group)
        cnt = counts[:, 0].astype(I32)
        n_tile = (cnt + tile_rows - 1) // tile_rows
        tile_end = jnp.cumsum(n_tile)
        seg_start = (tile_end - n_tile) * tile_rows
        route2 = jnp.swapaxes(route, 1, 2).reshape(t, SUBLANES)
        eid = route2[:, 0:TOP_K].astype(I32)
        rank = route2[:, TOP_K:2 * TOP_K].astype(I32)
        gate_w = route2[:, 2 * TOP_K:3 * TOP_K]
        experts = jnp.arange(n_experts, dtype=I32)
        pos = (jnp.sum(jnp.where(eid[..., None] == experts, seg_start, 0), axis=-1)
               + rank).reshape(t * TOP_K)
        n_used = tile_end[-1:].astype(I32)
        tile_live = jnp.minimum(jnp.arange(max_tiles, dtype=I32), n_used - 1)
        tile_exp = jnp.minimum(jnp.sum(tile_end[None, :] <= tile_live[:, None], axis=1),
                               n_experts - 1).astype(I32)
        xs = _dispatch(tile_end * tile_rows, cnt, n_used, pos, hf.reshape(t, d // 2),
                       rows_out=max_tiles * tile_rows, tile_rows=tile_rows)
        later = jnp.where((experts[None, :] > experts[:, None]) & (n_tile[None, :] > 0),
                          experts[None, :], n_experts)
        next_exp = jnp.min(later, axis=1)
        next_exp = jnp.where(next_exp < n_experts, next_exp, -1).astype(I32)
        tile_next = jnp.sum(jnp.where(tile_exp[:, None] == experts, next_exp, 0), axis=1)
        ys = _experts(tile_exp, tile_next.astype(I32), n_used, xs, w_expert_gu, w_expert_down,
                      layer=l, tile_rows=tile_rows)
        x = _combine(pos, x2.reshape(t, d), gate_w, ys).reshape(b, s, d)
    return x
```
